```python
import jax, jax.numpy as jnp
from jax import lax
import numpy as np

D_MODEL = 1024
BATCH = 8
SEQ = 2048
DEPTH = 2
DEC_BATCH = 32
DEC_SEQ = 8
PAST_LEN = 8192
PAGE_SIZE = 128

N_A = DEPTH // 2
N_B = DEPTH - N_A
D_CONV = D_MODEL
CONV_W = 3
N_HEADS = 16
HEAD_DIM = D_MODEL // N_HEADS
D_ATTN = N_HEADS * HEAD_DIM
Q_BLOCK = 128
EPS = 1e-6

kernel_name = 'yoco_shortconv_forgetting_attn_step'


def _rmsnorm(x, g):
    xf = x.astype(jnp.float32)
    r = lax.rsqrt(jnp.mean(xf * xf, axis=-1, keepdims=True) + EPS)
    return (xf * r).astype(x.dtype) * g


def _conv_layer(x, state, g, w_in, w_conv, w_out):
    b, c, u, z = jnp.split(_rmsnorm(x, g) @ w_in, 4, axis=-1)
    v = c * u
    ext = jnp.concatenate([state.astype(v.dtype), v], axis=1)
    L = x.shape[1]
    conv = ext[:, 0:L] * w_conv[0]
    for j in range(1, CONV_W):
        conv = conv + ext[:, j:j + L] * w_conv[j]
    y = (jax.nn.silu(z) * b * conv) @ w_out
    return x + y, ext[:, -(CONV_W - 1):]


def _shared_kv(h, g, w_kv, f_bias, g_k):
    N, L, _ = h.shape
    k, v, f = jnp.split(_rmsnorm(h, g) @ w_kv, [D_ATTN, 2 * D_ATTN], axis=-1)
    k = _rmsnorm(k.reshape(N, L, N_HEADS, HEAD_DIM), g_k)
    v = v.reshape(N, L, N_HEADS, HEAD_DIM)
    logf = jax.nn.log_sigmoid((f + f_bias).astype(jnp.float32))
    return k, v, logf


def _fox_attend(q, fq, qpos, k, v, fk, kpos):
    s = jnp.einsum('nqhd,nkhd->nhqk', q, k).astype(jnp.float32) * (HEAD_DIM ** -0.5)
    s = s + (jnp.transpose(fq, (0, 2, 1))[:, :, :, None]
             - jnp.transpose(fk, (0, 2, 1))[:, :, None, :])
    mask = kpos[None, :] <= qpos[:, None]
    s = jnp.where(mask[None, None], s, -jnp.inf)
    p = jax.nn.softmax(s, axis=-1)
    return jnp.einsum('nhqk,nkhd->nqhd', p.astype(v.dtype), v)


def _prompt_attend(q, k, v, F):
    N, L = q.shape[:2]
    kpos = jnp.arange(L)

    def blk(i):
        st = i * Q_BLOCK
        qb = lax.dynamic_slice_in_dim(q, st, Q_BLOCK, axis=1)
        fb = lax.dynamic_slice_in_dim(F, st, Q_BLOCK, axis=1)
        return _fox_attend(qb, fb, st + jnp.arange(Q_BLOCK), k, v, F, kpos)

    o = lax.map(blk, jnp.arange(L // Q_BLOCK))
    return jnp.moveaxis(o, 0, 1).reshape(q.shape)


def _fox_layer(x, attend, g, w_in, g_q, w_out):
    N, L, _ = x.shape
    q, z = jnp.split(_rmsnorm(x, g) @ w_in, 2, axis=-1)
    q = _rmsnorm(q.reshape(N, L, N_HEADS, HEAD_DIM), g_q)
    o = attend(q).reshape(N, L, D_ATTN)
    return x + (o * jax.nn.silu(z)) @ w_out


def setup_inputs(seed: int = 0) -> dict:
    key = jax.random.key(seed)
    ks = jax.random.split(key, 20)
    n_pages = PAST_LEN // PAGE_SIZE
    n_used = DEC_BATCH * n_pages
    n_pool = n_used + n_used // 4
    nrm = lambda k, shape, s=1.0: s * jax.random.normal(k, shape, jnp.float32)
    page_table = jax.random.permutation(ks[0], n_pool)[:n_used].reshape(DEC_BATCH, n_pages).astype(jnp.int32)
    return {
        'x_prompt': nrm(ks[1], (BATCH, SEQ, D_MODEL)),
        'x_sample': nrm(ks[2], (DEC_BATCH, DEC_SEQ, D_MODEL)),
        'state_conv': nrm(ks[3], (N_A, DEC_BATCH, CONV_W - 1, D_CONV)),
        'cache_k': nrm(ks[4], (n_pool, PAGE_SIZE, N_HEADS, HEAD_DIM)),
        'cache_v': nrm(ks[5], (n_pool, PAGE_SIZE, N_HEADS, HEAD_DIM)),
        'cache_logf': jax.nn.log_sigmoid(2.0 + nrm(ks[6], (n_pool, PAGE_SIZE, N_HEADS))),
        'page_table': page_table,
        'a_norm': 1.0 + nrm(ks[7], (N_A, D_MODEL), 0.02),
        'a_w_in': nrm(ks[8], (N_A, D_MODEL, 4 * D_CONV), D_MODEL ** -0.5),
        'a_conv': nrm(ks[9], (N_A, CONV_W, D_CONV), CONV_W ** -0.5),
        'a_w_out': nrm(ks[10], (N_A, D_CONV, D_MODEL), D_CONV ** -0.5),
        'kv_norm': 1.0 + nrm(ks[11], (D_MODEL,), 0.02),
        'kv_w': nrm(ks[12], (D_MODEL, 2 * D_ATTN + N_HEADS), D_MODEL ** -0.5),
        'kv_fbias': 2.0 + nrm(ks[13], (N_HEADS,), 0.1),
        'k_norm': 1.0 + nrm(ks[14], (HEAD_DIM,), 0.02),
        'b_norm': 1.0 + nrm(ks[15], (N_B, D_MODEL), 0.02),
        'b_w_in': nrm(ks[16], (N_B, D_MODEL, 2 * D_ATTN), D_MODEL ** -0.5),
        'q_norm': 1.0 + nrm(ks[17], (N_B, HEAD_DIM), 0.02),
        'b_w_out': nrm(ks[18], (N_B, D_ATTN, D_MODEL), D_ATTN ** -0.5),
    }


def reference(x_prompt, x_sample, state_conv, cache_k, cache_v, cache_logf, page_table,
              a_norm, a_w_in, a_conv, a_w_out, kv_norm, kv_w, kv_fbias, k_norm,
              b_norm, b_w_in, q_norm, b_w_out):
    nb, ts = x_sample.shape[:2]
    hp, hs = x_prompt, x_sample
    conv_p, conv_s = [], []
    prompt_attend = sample_attend = None
    for l in range(DEPTH):
        if l < N_A:
            zero_state = jnp.zeros((hp.shape[0], CONV_W - 1, D_CONV), hp.dtype)
            hp, sp = _conv_layer(hp, zero_state, a_norm[l], a_w_in[l], a_conv[l], a_w_out[l])
            hs, ss = _conv_layer(hs, state_conv[l], a_norm[l], a_w_in[l], a_conv[l], a_w_out[l])
            conv_p.append(sp)
            conv_s.append(ss)
            continue
        if l == N_A:
            k_p, v_p, lf_p = _shared_kv(hp, kv_norm, kv_w, kv_fbias, k_norm)
            k_s, v_s, lf_s = _shared_kv(hs, kv_norm, kv_w, kv_fbias, k_norm)
            F_p = jnp.cumsum(lf_p, axis=1)
            prompt_attend = lambda q: _prompt_attend(q, k_p, v_p, F_p)
            k_past = cache_k[page_table].reshape(nb, -1, N_HEADS, HEAD_DIM)
            v_past = cache_v[page_table].reshape(nb, -1, N_HEADS, HEAD_DIM)
            lf_past = cache_logf[page_table].reshape(nb, -1, N_HEADS).astype(jnp.float32)
            past = k_past.shape[1]
            k_all = jnp.concatenate([k_past, k_s.astype(k_past.dtype)], axis=1)
            v_all = jnp.concatenate([v_past, v_s.astype(v_past.dtype)], axis=1)
            F_all = jnp.cumsum(jnp.concatenate([lf_past, lf_s], axis=1), axis=1)
            kpos_all = jnp.arange(past + ts)
            qpos_s = past + jnp.arange(ts)
            sample_attend = lambda q: _fox_attend(q, F_all[:, past:], qpos_s,
                                                  k_all, v_all, F_all, kpos_all)
        j = l - N_A
        hp = _fox_layer(hp, prompt_attend, b_norm[j], b_w_in[j], q_norm[j], b_w_out[j])
        hs = _fox_layer(hs, sample_attend, b_norm[j], b_w_in[j], q_norm[j], b_w_out[j])
    conv_prompt = jnp.stack(conv_p, axis=0)
    conv_sample = jnp.stack(conv_s, axis=0)
    return (hp, hs, conv_prompt, conv_sample, k_p, v_p, lf_p, k_s, v_s, lf_s)
```

```python
import jax
import jax.numpy as jnp
from jax import lax
from jax.experimental import pallas as pl
from jax.experimental.pallas import tpu as pltpu

F32 = jnp.float32
BF16 = jnp.bfloat16

D_MODEL = 1024
N_HEADS = 16
HEAD_DIM = 64
CONV_W = 3
EPS = 1e-6
LOG2E = 1.4426950408889634
NEG_BIG = -1e30

LANES = 128
SUBLANES = 8
AUG = LANES // N_HEADS
VMEM_LIMIT = 56 * 1024 * 1024

PRE_T = 256
ATT_T = 256
OUT_T = 512
PAGES_PER_STEP = 8


def _dot(a, b):
    return jnp.dot(a, b, preferred_element_type=F32)


def _dot_nt(a, b):
    return lax.dot_general(a, b, (((1,), (1,)), ((), ())), preferred_element_type=F32)


def _split3(x):
    hi = x.astype(BF16)
    r1 = x - hi.astype(F32)
    mid = r1.astype(BF16)
    r2 = r1 - mid.astype(F32)
    return hi, mid, r2.astype(BF16)


def _dot_exact_rhs(m_bf16, x_f32):
    hi, mid, lo = _split3(x_f32)
    return _dot(m_bf16, hi) + _dot(m_bf16, mid) + _dot(m_bf16, lo)


def _dot_exact_lhs(x_f32, m_bf16):
    hi, mid, lo = _split3(x_f32)
    return _dot(hi, m_bf16) + _dot(mid, m_bf16) + _dot(lo, m_bf16)


def _rms(x, g):
    r = lax.rsqrt(jnp.mean(x * x, axis=-1, keepdims=True) + EPS)
    return x * r * g


def _head_rms(x, g128):
    lane = lax.broadcasted_iota(jnp.int32, (1, LANES), 1)
    lo_half = lane < HEAD_DIM
    outs = []
    for p in range(x.shape[1] // LANES):
        xp = x[:, p * LANES:(p + 1) * LANES]
        sq = xp * xp
        s_lo = jnp.sum(jnp.where(lo_half, sq, 0.0), axis=-1, keepdims=True)
        s_hi = jnp.sum(jnp.where(lo_half, 0.0, sq), axis=-1, keepdims=True)
        r = jnp.where(lo_half, lax.rsqrt(s_lo / HEAD_DIM + EPS), lax.rsqrt(s_hi / HEAD_DIM + EPS))
        outs.append(xp * r * g128)
    return jnp.concatenate(outs, axis=1)


def _head_rms_fm(xT, gT):
    n, T = xT.shape
    x3 = xT.reshape(n // HEAD_DIM, HEAD_DIM, T)
    r = lax.rsqrt(jnp.mean(x3 * x3, axis=1, keepdims=True) + EPS)
    return (x3 * r).reshape(n, T) * gT


def _silu(z):
    return z * jax.nn.sigmoid(z)


def _log_sigmoid(x):
    return jnp.minimum(x, 0.0) - jnp.log1p(jnp.exp(-jnp.abs(x)))


def _bias_select(F, j, key_side):
    hi, mid, lo = (t.astype(F32) for t in _split3(F))
    if key_side:
        out = jnp.where(j < 3, 1.0, jnp.where(j == 3, -hi, jnp.where(j == 4, -mid, jnp.where(j == 5, -lo, 0.0))))
    else:
        out = jnp.where(j == 0, hi, jnp.where(j == 1, mid, jnp.where(j == 2, lo, jnp.where(j < 6, 1.0, 0.0))))
    return out.astype(BF16)


def _conv_layer(x, prev1, prev2, m1, m2, w):
    D = D_MODEL
    xn = _rms(x, w['a_norm'][...]).astype(BF16)
    w_in = w['a_w_in']
    v = _dot(xn, w_in[:, D:2 * D]) * _dot(xn, w_in[:, 2 * D:3 * D])
    v1 = jnp.where(m1, prev1, pltpu.roll(v, 1, axis=0))
    v2 = jnp.where(m2, prev2, pltpu.roll(v, 2, axis=0))
    cw = w['a_conv'][...]
    conv = v2 * cw[0:1] + v1 * cw[1:2] + v * cw[2:3]
    b = _dot(xn, w_in[:, 0:D])
    z = _dot(xn, w_in[:, 3 * D:4 * D])
    g = (_silu(z) * b * conv).astype(BF16)
    return v, x + _dot(g, w['a_w_out'][...])


def _fox_proj(h1, w):
    D = D_MODEL
    xn = _rms(h1, w['b_norm'][...]).astype(BF16)
    b_w_in = w['b_w_in']
    q = _head_rms(_dot(xn, b_w_in[:, 0:D]), w['q_norm'][...]) * (LOG2E * HEAD_DIM ** -0.5)
    gate = _silu(_dot(xn, b_w_in[:, D:2 * D]))
    return q.astype(BF16), gate.astype(BF16)


_W_COMMON = ('a_norm', 'a_w_in', 'a_conv', 'a_w_out', 'kv_norm', 'b_norm', 'b_w_in', 'q_norm')
_W_PROMPT = _W_COMMON + ('kv_wT', 'fbias_col', 'k_norm_col', 'tri_u')
_W_SAMPLE = _W_COMMON + ('kv_w', 'fbias_row', 'k_norm_row')


def _pre_prompt_kernel(x_ref, *refs):
    nw = len(_W_PROMPT)
    w = dict(zip(_W_PROMPT, refs[:nw]))
    (h1_ref, kT_ref, vT_ref, lfT_ref, q_ref, qf_ref, kfT_ref, gate_ref, tail_ref, vcarry, fcarry) = refs[nw:]
    T = x_ref.shape[1]
    D = D_MODEL
    reps = T // LANES

    @pl.when(pl.program_id(1) == 0)
    def _():
        vcarry[...] = jnp.zeros_like(vcarry)
        fcarry[...] = jnp.zeros_like(fcarry)

    row = lax.broadcasted_iota(jnp.int32, (T, 1), 0)
    prev = vcarry[...]
    pm1 = prev[SUBLANES - 1:SUBLANES]
    pm2 = prev[SUBLANES - 2:SUBLANES - 1]
    v, h1 = _conv_layer(x_ref[0], pm1, jnp.where(row == 0, pm2, pm1), row == 0, row < 2, w)
    tail = v[T - SUBLANES:T]
    vcarry[...] = tail
    tail_ref[0] = tail
    h1_ref[0] = h1

    hn = _rms(h1, w['kv_norm'][...]).astype(BF16)
    kvT = w['kv_wT']
    kT_ref[0] = _head_rms_fm(_dot_nt(kvT[0:D, :], hn), pltpu.repeat(w['k_norm_col'][...], reps, axis=1))
    vT_ref[0] = _dot_nt(kvT[D:2 * D, :], hn)
    fT = _dot_nt(kvT[2 * D:2 * D + LANES + N_HEADS, :], hn) + pltpu.repeat(w['fbias_col'][...], reps, axis=1)
    lfT = _log_sigmoid(fT)
    lfT_ref[0] = lfT[LANES:LANES + N_HEADS]
    FT = _dot_exact_lhs(lfT[0:LANES], w['tri_u'][...]) + pltpu.repeat(fcarry[...], reps, axis=1)
    fcarry[...] = jnp.broadcast_to(FT[:, T - 1:T], fcarry.shape)
    Fp = FT * LOG2E
    kfT_ref[0] = _bias_select(Fp, lax.broadcasted_iota(jnp.int32, (LANES, 1), 0) % AUG, True)
    qf_ref[0] = _bias_select(Fp.T, lax.broadcasted_iota(jnp.int32, (1, LANES), 1) % AUG, False)

    q, gate = _fox_proj(h1, w)
    q_ref[0] = q
    gate_ref[0] = gate


def _pre_sample_kernel(x_ref, p1_ref, p2_ref, *refs):
    nw = len(_W_SAMPLE)
    w = dict(zip(_W_SAMPLE, refs[:nw]))
    h1_ref, k_ref, v_ref, lf_ref, q_ref, gate_ref, cv_ref = refs[nw:]
    D = D_MODEL
    t = lax.broadcasted_iota(jnp.int32, (x_ref.shape[0], 1), 0) % SUBLANES
    v, h1 = _conv_layer(x_ref[...], p1_ref[...], p2_ref[...], t == 0, t < 2, w)
    cv_ref[...] = v
    h1_ref[...] = h1
    hn = _rms(h1, w['kv_norm'][...]).astype(BF16)
    kv_w = w['kv_w']
    k_ref[...] = _head_rms(_dot(hn, kv_w[:, 0:D]), w['k_norm_row'][...])
    v_ref[...] = _dot(hn, kv_w[:, D:2 * D])
    lfe = _log_sigmoid(_dot(hn, kv_w[:, 2 * D:2 * D + LANES]) + w['fbias_row'][...])
    lf_ref[...] = lfe[:, 0:N_HEADS]
    q, gate = _fox_proj(h1, w)
    q_ref[...] = q
    gate_ref[...] = gate


def _pre_prompt(x, weights):
    B, S, D = x.shape
    T = PRE_T
    tok = lambda d: pl.BlockSpec((1, T, d), lambda b, j: (b, j, 0))
    feat = lambda n: pl.BlockSpec((1, n, T), lambda b, j: (b, 0, j))
    out_shape = [
        jax.ShapeDtypeStruct((B, S, D), F32),
        jax.ShapeDtypeStruct((B, D, S), F32),
        jax.ShapeDtypeStruct((B, D, S), F32),
        jax.ShapeDtypeStruct((B, N_HEADS, S), F32),
        jax.ShapeDtypeStruct((B, S, D), BF16),
        jax.ShapeDtypeStruct((B, S, LANES), BF16),
        jax.ShapeDtypeStruct((B, LANES, S), BF16),
        jax.ShapeDtypeStruct((B, S, D), BF16),
        jax.ShapeDtypeStruct((B, SUBLANES, D), F32),
    ]
    out_specs = [tok(D), feat(D), feat(D), feat(N_HEADS), tok(D), tok(LANES), feat(LANES), tok(D),
                 pl.BlockSpec((1, SUBLANES, D), lambda b, j: (b, 0, 0))]
    const = lambda b, j: (0, 0)
    w_specs = [pl.BlockSpec(w.shape, const, pipeline_mode=pl.Buffered(1)) for w in weights]
    return pl.pallas_call(
        _pre_prompt_kernel,
        grid=(B, S // T),
        in_specs=[tok(D)] + w_specs,
        out_specs=out_specs,
        out_shape=out_shape,
        scratch_shapes=[pltpu.VMEM((SUBLANES, D), F32), pltpu.VMEM((LANES, LANES), F32)],
        compiler_params=pltpu.CompilerParams(
            dimension_semantics=("arbitrary", "arbitrary"), vmem_limit_bytes=VMEM_LIMIT),
        name="pre_prompt",
    )(x, *weights)


def _pre_sample(x2, p1, p2, weights):
    R, D = x2.shape
    out_shape = [
        jax.ShapeDtypeStruct((R, D), F32),
        jax.ShapeDtypeStruct((R, D), F32),
        jax.ShapeDtypeStruct((R, D), F32),
        jax.ShapeDtypeStruct((R, N_HEADS), F32),
        jax.ShapeDtypeStruct((R, D), BF16),
        jax.ShapeDtypeStruct((R, D), BF16),
        jax.ShapeDtypeStruct((R, D), F32),
    ]
    return pl.pallas_call(
        _pre_sample_kernel,
        out_shape=out_shape,
        compiler_params=pltpu.CompilerParams(vmem_limit_bytes=VMEM_LIMIT),
        name="pre_sample",
    )(x2, p1, p2, *weights)


def _attn_kernel(q_ref, qf_ref, kT_ref, kfT_ref, vT_ref, gate_ref, o_ref, kbuf, vbuf, m_scr, l_scr, acc_scr):
    T = q_ref.shape[1]
    hp = pl.program_id(1)
    i = pl.program_id(2)

    @pl.when(i == 0)
    def _():
        for kt in range(kbuf.shape[0]):
            cols = slice(kt * T, (kt + 1) * T)
            kbuf[kt, 0:LANES, :] = kT_ref[0, :, cols].astype(BF16)
            kbuf[kt, LANES:2 * LANES, :] = kfT_ref[0, :, cols]
            vbuf[kt] = vT_ref[0, :, cols].astype(BF16)

    lane = lax.broadcasted_iota(jnp.int32, (1, LANES), 1)
    lo_half = lane < HEAD_DIM
    q2 = q_ref[0]
    qf = qf_ref[0]
    zero = jnp.zeros_like(q2)
    lhs = jnp.concatenate([
        jnp.concatenate([jnp.where(lo_half, q2, zero), jnp.where(lane // AUG == 2 * hp, qf, zero)], axis=1),
        jnp.concatenate([jnp.where(lo_half, zero, q2), jnp.where(lane // AUG == 2 * hp + 1, qf, zero)], axis=1),
    ], axis=0)

    m_scr[...] = jnp.full_like(m_scr, NEG_BIG)
    l_scr[...] = jnp.zeros_like(l_scr)
    acc_scr[...] = jnp.zeros_like(acc_scr)

    def step(kt, masked):
        s = _dot(lhs, kbuf[kt])
        if masked:
            r = lax.broadcasted_iota(jnp.int32, s.shape, 0)
            c = lax.broadcasted_iota(jnp.int32, s.shape, 1)
            s = jnp.where(c <= jnp.where(r >= T, r - T, r), s, NEG_BIG)
        m_prev = m_scr[...]
        m_new = jnp.maximum(m_prev, jnp.max(s, axis=1, keepdims=True))
        alpha = jnp.exp2(m_prev - m_new)
        p = jnp.exp2(s - pltpu.repeat(m_new, T // LANES, axis=1))
        l_scr[...] = alpha * l_scr[...] + jnp.sum(p, axis=1, keepdims=True)
        acc_scr[...] = alpha * acc_scr[...] + _dot_nt(p.astype(BF16), vbuf[kt])
        m_scr[...] = m_new

    def body(kt, carry):
        step(kt, False)
        return carry

    lax.fori_loop(0, i, body, 0)
    step(i, True)

    o = acc_scr[...] / l_scr[...]
    o = jnp.where(lo_half, o[0:T], o[T:2 * T])
    o_ref[0] = (o * gate_ref[0].astype(F32)).astype(BF16)


def _attention(q, qf, kT, kfT, vT, gate):
    B, S, D = q.shape
    T = ATT_T
    qspec = pl.BlockSpec((1, T, LANES), lambda b, hp, i: (b, i, hp))
    kspec = pl.BlockSpec((1, LANES, S), lambda b, hp, i: (b, hp, 0))
    return pl.pallas_call(
        _attn_kernel,
        grid=(B, D // LANES, S // T),
        in_specs=[qspec, pl.BlockSpec((1, T, LANES), lambda b, hp, i: (b, i, 0)),
                  kspec, pl.BlockSpec((1, LANES, S), lambda b, hp, i: (b, 0, 0)), kspec, qspec],
        out_specs=qspec,
        out_shape=jax.ShapeDtypeStruct((B, S, D), BF16),
        scratch_shapes=[pltpu.VMEM((S // T, 2 * LANES, T), BF16), pltpu.VMEM((S // T, LANES, T), BF16),
                        pltpu.VMEM((2 * T, LANES), F32), pltpu.VMEM((2 * T, LANES), F32),
                        pltpu.VMEM((2 * T, LANES), F32)],
        compiler_params=pltpu.CompilerParams(
            dimension_semantics=("arbitrary", "arbitrary", "arbitrary"), vmem_limit_bytes=VMEM_LIMIT),
        name="prompt_attn",
    )(q, qf, kT, kfT, vT, gate)


def _sattn_kernel(pt_ref, *refs):
    P = PAGES_PER_STEP
    kpages = refs[0:P]
    vpages = refs[P:2 * P]
    lpages = refs[2 * P:3 * P]
    (knew_ref, vnew_ref, lnew_ref, q_ref, gate_ref, tri_ref,
     o_ref, qblk, gcarry, m_scr, l_scr, acc_scr) = refs[3 * P:]
    c = pl.program_id(1)
    nq = q_ref.shape[1]
    R = LANES
    PG = LANES
    D = D_MODEL

    @pl.when(c == 0)
    def _():
        q = q_ref[0]
        rows = lax.broadcasted_iota(jnp.int32, (R, D), 0)
        cols = lax.broadcasted_iota(jnp.int32, (R, D), 1)
        tiled = jnp.concatenate([q] * (R // nq), axis=0)
        qblk[...] = jnp.where(cols // HEAD_DIM == rows // nq, tiled, jnp.zeros_like(tiled))
        gcarry[...] = jnp.zeros_like(gcarry)
        m_scr[...] = jnp.full_like(m_scr, NEG_BIG)
        l_scr[...] = jnp.zeros_like(l_scr)
        acc_scr[...] = jnp.zeros_like(acc_scr)

    tri_u = tri_ref[...]

    def bias_rows(lf_pages):
        cs = _dot_exact_lhs(jnp.concatenate(lf_pages, axis=0), tri_u)
        carry = gcarry[...]
        out = []
        for j in range(len(lf_pages)):
            G = cs[j * N_HEADS:(j + 1) * N_HEADS] + carry
            carry = jnp.broadcast_to(G[:, PG - 1:PG], carry.shape)
            out.append(jnp.concatenate(
                [jnp.broadcast_to(G[h:h + 1], (nq, PG)) for h in range(N_HEADS)], axis=0))
        gcarry[...] = carry
        return jnp.concatenate(out, axis=1) if len(out) > 1 else out[0]

    def update(s, vT):
        m_prev = m_scr[...]
        m_new = jnp.maximum(m_prev, jnp.max(s, axis=1, keepdims=True))
        alpha = jnp.exp2(m_prev - m_new)
        p = jnp.exp2(s - pltpu.repeat(m_new, s.shape[1] // LANES, axis=1))
        l_scr[...] = alpha * l_scr[...] + jnp.sum(p, axis=1, keepdims=True)
        acc_scr[...] = jnp.concatenate([alpha.T] * (D // LANES), axis=0) * acc_scr[...] + _dot_nt(vT, p.astype(BF16))
        m_scr[...] = m_new

    kT = jnp.concatenate([kpages[j][0].astype(BF16) for j in range(P)], axis=1)
    vT = jnp.concatenate([vpages[j][0].astype(BF16) for j in range(P)], axis=1)
    s = _dot(qblk[...], kT) - bias_rows([lpages[j][0] for j in range(P)]) * LOG2E
    update(s, vT)

    @pl.when(c == pl.num_programs(1) - 1)
    def _():
        s = _dot(qblk[...], knew_ref[0].astype(BF16)) - bias_rows([lnew_ref[0]]) * LOG2E
        key = lax.broadcasted_iota(jnp.int32, s.shape, 1)
        qi = lax.broadcasted_iota(jnp.int32, s.shape, 0) % nq
        update(jnp.where(key <= qi, s, NEG_BIG), vnew_ref[0].astype(BF16))

        linv = 1.0 / l_scr[...]
        oT = acc_scr[...] * jnp.concatenate([linv.T] * (D // LANES), axis=0)
        rows = lax.broadcasted_iota(jnp.int32, oT.shape, 0)
        cols = lax.broadcasted_iota(jnp.int32, oT.shape, 1)
        oT = jnp.where(rows // HEAD_DIM == cols // nq, oT, 0.0)
        shift = nq
        while shift < LANES:
            oT = oT + pltpu.roll(oT, shift, axis=1)
            shift *= 2
        o = oT.T[0:nq]
        o_ref[0] = (o * gate_ref[0].astype(F32)).astype(BF16)


def _sample_attention(page_table, ckT, cvT, clT, knewT, vnewT, lnewT, q, gate, tri_u):
    NB, NQ, D = q.shape
    n_pages = page_table.shape[1]
    P = PAGES_PER_STEP
    PG = ckT.shape[2]

    def page_spec(rows, j):
        return pl.BlockSpec((1, rows, PG), lambda b, c, pt: (pt[b, c * P + j], 0, 0))

    per_seq = lambda rows, width: pl.BlockSpec((1, rows, width), lambda b, c, pt: (b, 0, 0))
    in_specs = ([page_spec(D, j) for j in range(P)] + [page_spec(D, j) for j in range(P)]
                + [page_spec(N_HEADS, j) for j in range(P)]
                + [per_seq(D, PG), per_seq(D, PG), per_seq(N_HEADS, PG), per_seq(NQ, D), per_seq(NQ, D),
                   pl.BlockSpec(tri_u.shape, lambda b, c, pt: (0, 0))])
    grid_spec = pltpu.PrefetchScalarGridSpec(
        num_scalar_prefetch=1,
        grid=(NB, n_pages // P),
        in_specs=in_specs,
        out_specs=per_seq(NQ, D),
        scratch_shapes=[pltpu.VMEM((LANES, D), BF16),
                        pltpu.VMEM((N_HEADS, LANES), F32),
                        pltpu.VMEM((LANES, LANES), F32), pltpu.VMEM((LANES, LANES), F32),
                        pltpu.VMEM((D, LANES), F32)],
    )
    return pl.pallas_call(
        _sattn_kernel,
        grid_spec=grid_spec,
        out_shape=jax.ShapeDtypeStruct((NB, NQ, D), BF16),
        compiler_params=pltpu.CompilerParams(
            dimension_semantics=("arbitrary", "arbitrary"), vmem_limit_bytes=VMEM_LIMIT),
        name="sample_attn",
    )(page_table, *([ckT] * P), *([cvT] * P), *([clT] * P), knewT, vnewT, lnewT, q, gate, tri_u)


def _outproj_kernel(og_ref, h_ref, w_ref, y_ref):
    y_ref[...] = h_ref[...] + _dot(og_ref[...], w_ref[...])


def _outproj(og, h, w):
    R, D = h.shape
    T = min(OUT_T, R)
    row = pl.BlockSpec((T, D), lambda i: (i, 0))
    return pl.pallas_call(
        _outproj_kernel,
        grid=(R // T,),
        in_specs=[row, row, pl.BlockSpec(w.shape, lambda i: (0, 0), pipeline_mode=pl.Buffered(1))],
        out_specs=row,
        out_shape=jax.ShapeDtypeStruct((R, D), F32),
        compiler_params=pltpu.CompilerParams(
            dimension_semantics=("arbitrary",), vmem_limit_bytes=VMEM_LIMIT),
        name="outproj",
    )(og, h, w)


def _upper_tri(n):
    return (jnp.arange(n)[:, None] <= jnp.arange(n)[None, :]).astype(BF16)


def kernel(x_prompt, x_sample, state_conv, cache_k, cache_v, cache_logf, page_table,
           a_norm, a_w_in, a_conv, a_w_out, kv_norm, kv_w, kv_fbias, k_norm,
           b_norm, b_w_in, q_norm, b_w_out):
    B, S, D = x_prompt.shape
    NB, NQ, _ = x_sample.shape
    n_pool, PG = cache_k.shape[0], cache_k.shape[1]
    H, hd = N_HEADS, HEAD_DIM
    assert a_norm.shape[0] == 1 and b_norm.shape[0] == 1, "one conv layer then one attention layer"
    assert D == D_MODEL and NQ == SUBLANES and PG == LANES

    common = [a_norm[0][None], a_w_in[0].astype(BF16), a_conv[0], a_w_out[0].astype(BF16),
              kv_norm[None], b_norm[0][None], b_w_in[0].astype(BF16), jnp.tile(q_norm[0], 2)[None]]
    kv_wT = kv_w.T
    kv_wT_ext = jnp.concatenate([kv_wT[:2 * D], jnp.repeat(kv_wT[2 * D:], AUG, axis=0), kv_wT[2 * D:]], axis=0)
    fbias_col = jnp.broadcast_to(jnp.concatenate([jnp.repeat(kv_fbias, AUG), kv_fbias])[:, None],
                                 (LANES + H, LANES))
    k_norm_col = jnp.broadcast_to(jnp.tile(k_norm, H)[:, None], (D, LANES))
    w_prompt = common + [kv_wT_ext.astype(BF16), fbias_col, k_norm_col, _upper_tri(PRE_T)]
    kv_w_row = jnp.concatenate([kv_w, jnp.zeros((D, LANES - H), kv_w.dtype)], axis=1)
    fbias_row = jnp.concatenate([kv_fbias, jnp.zeros((LANES - H,), kv_fbias.dtype)])[None]
    w_sample = common + [kv_w_row.astype(BF16), fbias_row, jnp.tile(k_norm, 2)[None]]
    w_out_b = b_w_out[0].astype(BF16)

    (h1, kT, vT, lfT, q, qf, kfT, gate, tail) = _pre_prompt(x_prompt, w_prompt)
    og = _attention(q, qf, kT, kfT, vT, gate)
    y_prompt = _outproj(og.reshape(B * S, D), h1.reshape(B * S, D), w_out_b).reshape(B, S, D)

    st = state_conv[0]
    p1 = jnp.pad(st[:, 1:2], ((0, 0), (0, NQ - 1), (0, 0))).reshape(NB * NQ, D)
    p2 = jnp.pad(st, ((0, 0), (0, NQ - 2), (0, 0))).reshape(NB * NQ, D)
    (h1s, k_s, v_s, lf_s, qs, gate_s, cv_s) = _pre_sample(x_sample.reshape(NB * NQ, D), p1, p2, w_sample)
    new_page = lambda a: jnp.pad(a.reshape(NB, NQ, -1), ((0, 0), (0, PG - NQ), (0, 0))).transpose(0, 2, 1)
    ogs = _sample_attention(
        page_table,
        cache_k.transpose(0, 2, 3, 1).reshape(n_pool, D, PG),
        cache_v.transpose(0, 2, 3, 1).reshape(n_pool, D, PG),
        cache_logf.transpose(0, 2, 1),
        new_page(k_s), new_page(v_s), new_page(lf_s),
        qs.reshape(NB, NQ, D), gate_s.reshape(NB, NQ, D), _upper_tri(PG))
    y_sample = _outproj(ogs.reshape(NB * NQ, D), h1s, w_out_b).reshape(NB, NQ, D)

    conv_prompt = tail[:, SUBLANES - (CONV_W - 1):][None]
    conv_sample = cv_s.reshape(NB, NQ, D)[:, NQ - (CONV_W - 1):][None]
    to_bshd = lambda t: t.reshape(B, H, hd, S).transpose(0, 3, 1, 2)
    return (y_prompt, y_sample, conv_prompt, conv_sample,
            to_bshd(kT), to_bshd(vT), lfT.transpose(0, 2, 1),
            k_s.reshape(NB, NQ, H, hd), v_s.reshape(NB, NQ, H, hd), lf_s.reshape(NB, NQ, H))
```

```python
import jax
import jax.numpy as jnp
from jax import lax
from jax.experimental import pallas as pl
from jax.experimental.pallas import tpu as pltpu

F32 = jnp.float32
BF16 = jnp.bfloat16

D_MODEL = 1024
N_HEADS = 16
HEAD_DIM = 64
CONV_W = 3
EPS = 1e-6
LOG2E = 1.4426950408889634
NEG_BIG = -1e30

LANES = 128
SUBLANES = 8
AUG = LANES // N_HEADS
VMEM_LIMIT = 56 * 1024 * 1024

PRE_T = 256
ATT_T = 256
OUT_T = 512
PAGES_PER_STEP = 8


def _dot(a, b):
    return jnp.dot(a, b, preferred_element_type=F32)


def _dot_nt(a, b):
    return lax.dot_general(a, b, (((1,), (1,)), ((), ())), preferred_element_type=F32)


def _split3(x):
    hi = x.astype(BF16)
    r1 = x - hi.astype(F32)
    mid = r1.astype(BF16)
    r2 = r1 - mid.astype(F32)
    return hi, mid, r2.astype(BF16)


def _dot_exact_rhs(m_bf16, x_f32):
    hi, mid, lo = _split3(x_f32)
    return _dot(m_bf16, hi) + _dot(m_bf16, mid) + _dot(m_bf16, lo)


def _dot_exact_lhs(x_f32, m_bf16):
    hi, mid, lo = _split3(x_f32)
    return _dot(hi, m_bf16) + _dot(mid, m_bf16) + _dot(lo, m_bf16)


def _rms(x, g):
    r = lax.rsqrt(jnp.mean(x * x, axis=-1, keepdims=True) + EPS)
    return x * r * g


def _head_rms(x, g128):
    lane = lax.broadcasted_iota(jnp.int32, (1, LANES), 1)
    lo_half = lane < HEAD_DIM
    outs = []
    for p in range(x.shape[1] // LANES):
        xp = x[:, p * LANES:(p + 1) * LANES]
        sq = xp * xp
        s_lo = jnp.sum(jnp.where(lo_half, sq, 0.0), axis=-1, keepdims=True)
        s_hi = jnp.sum(jnp.where(lo_half, 0.0, sq), axis=-1, keepdims=True)
        r = jnp.where(lo_half, lax.rsqrt(s_lo / HEAD_DIM + EPS), lax.rsqrt(s_hi / HEAD_DIM + EPS))
        outs.append(xp * r * g128)
    return jnp.concatenate(outs, axis=1)


def _head_rms_fm(xT, gT):
    n, T = xT.shape
    x3 = xT.reshape(n // HEAD_DIM, HEAD_DIM, T)
    r = lax.rsqrt(jnp.mean(x3 * x3, axis=1, keepdims=True) + EPS)
    return (x3 * r).reshape(n, T) * gT


def _silu(z):
    return z * jax.nn.sigmoid(z)


def _log_sigmoid(x):
    return jnp.minimum(x, 0.0) - jnp.log1p(jnp.exp(-jnp.abs(x)))


def _bias_select(F, j, key_side):
    hi, mid, lo = (t.astype(F32) for t in _split3(F))
    if key_side:
        out = jnp.where(j < 3, 1.0, jnp.where(j == 3, -hi, jnp.where(j == 4, -mid, jnp.where(j == 5, -lo, 0.0))))
    else:
        out = jnp.where(j == 0, hi, jnp.where(j == 1, mid, jnp.where(j == 2, lo, jnp.where(j < 6, 1.0, 0.0))))
    return out.astype(BF16)


def _conv_layer(x, prev1, prev2, m1, m2, w):
    D = D_MODEL
    xn = _rms(x, w['a_norm'][...]).astype(BF16)
    w_in = w['a_w_in']
    v = _dot(xn, w_in[:, D:2 * D]) * _dot(xn, w_in[:, 2 * D:3 * D])
    v1 = jnp.where(m1, prev1, pltpu.roll(v, 1, axis=0))
    v2 = jnp.where(m2, prev2, pltpu.roll(v, 2, axis=0))
    cw = w['a_conv'][...]
    conv = v2 * cw[0:1] + v1 * cw[1:2] + v * cw[2:3]
    b = _dot(xn, w_in[:, 0:D])
    z = _dot(xn, w_in[:, 3 * D:4 * D])
    g = (_silu(z) * b * conv).astype(BF16)
    return v, x + _dot(g, w['a_w_out'][...])


def _fox_proj(h1, w):
    D = D_MODEL
    xn = _rms(h1, w['b_norm'][...]).astype(BF16)
    b_w_in = w['b_w_in']
    q = _head_rms(_dot(xn, b_w_in[:, 0:D]), w['q_norm'][...]) * (LOG2E * HEAD_DIM ** -0.5)
    gate = _silu(_dot(xn, b_w_in[:, D:2 * D]))
    return q.astype(BF16), gate.astype(BF16)


_W_COMMON = ('a_norm', 'a_w_in', 'a_conv', 'a_w_out', 'kv_norm', 'b_norm')
_W_PROMPT = _W_COMMON + ('kv_wT', 'fbias_col', 'k_norm_col', 'tri_u', 'b_w_inT', 'q_norm_col')
_W_SAMPLE = _W_COMMON + ('kv_w', 'fbias_row', 'k_norm_row', 'b_w_in', 'q_norm')


def _pre_prompt_kernel(x_ref, *refs):
    nw = len(_W_PROMPT)
    w = dict(zip(_W_PROMPT, refs[:nw]))
    (h1_ref, kT_ref, vT_ref, lfT_ref, qT_ref, qfT_ref, kfT_ref, gateT_ref, tail_ref, vcarry, fcarry) = refs[nw:]
    T = x_ref.shape[1]
    D = D_MODEL
    reps = T // LANES

    @pl.when(pl.program_id(1) == 0)
    def _():
        vcarry[...] = jnp.zeros_like(vcarry)
        fcarry[...] = jnp.zeros_like(fcarry)

    row = lax.broadcasted_iota(jnp.int32, (T, 1), 0)
    prev = vcarry[...]
    pm1 = prev[SUBLANES - 1:SUBLANES]
    pm2 = prev[SUBLANES - 2:SUBLANES - 1]
    v, h1 = _conv_layer(x_ref[0], pm1, jnp.where(row == 0, pm2, pm1), row == 0, row < 2, w)
    tail = v[T - SUBLANES:T]
    vcarry[...] = tail
    tail_ref[0] = tail
    h1_ref[0] = h1

    hn = _rms(h1, w['kv_norm'][...]).astype(BF16)
    kvT = w['kv_wT']
    kT_ref[0] = _head_rms_fm(_dot_nt(kvT[0:D, :], hn), pltpu.repeat(w['k_norm_col'][...], reps, axis=1))
    vT_ref[0] = _dot_nt(kvT[D:2 * D, :], hn)
    fT = _dot_nt(kvT[2 * D:2 * D + LANES + N_HEADS, :], hn) + pltpu.repeat(w['fbias_col'][...], reps, axis=1)
    lfT = _log_sigmoid(fT)
    lfT_ref[0] = lfT[LANES:LANES + N_HEADS]
    FT = _dot_exact_lhs(lfT[0:LANES], w['tri_u'][...]) + pltpu.repeat(fcarry[...], reps, axis=1)
    fcarry[...] = jnp.broadcast_to(FT[:, T - 1:T], fcarry.shape)
    Fp = FT * LOG2E
    slot = lax.broadcasted_iota(jnp.int32, (LANES, 1), 0) % AUG
    kfT_ref[0] = _bias_select(Fp, slot, True)
    qfT_ref[0] = _bias_select(Fp, slot, False)

    xn2 = _rms(h1, w['b_norm'][...]).astype(BF16)
    b_w_inT = w['b_w_inT']
    qT = _head_rms_fm(_dot_nt(b_w_inT[0:D, :], xn2), pltpu.repeat(w['q_norm_col'][...], reps, axis=1))
    qT_ref[0] = (qT * (LOG2E * HEAD_DIM ** -0.5)).astype(BF16)
    gateT_ref[0] = _silu(_dot_nt(b_w_inT[D:2 * D, :], xn2)).astype(BF16)


def _pre_sample_kernel(x_ref, p1_ref, p2_ref, *refs):
    nw = len(_W_SAMPLE)
    w = dict(zip(_W_SAMPLE, refs[:nw]))
    h1_ref, k_ref, v_ref, lf_ref, q_ref, gate_ref, cv_ref = refs[nw:]
    D = D_MODEL
    t = lax.broadcasted_iota(jnp.int32, (x_ref.shape[0], 1), 0) % SUBLANES
    v, h1 = _conv_layer(x_ref[...], p1_ref[...], p2_ref[...], t == 0, t < 2, w)
    cv_ref[...] = v
    h1_ref[...] = h1
    hn = _rms(h1, w['kv_norm'][...]).astype(BF16)
    kv_w = w['kv_w']
    k_ref[...] = _head_rms(_dot(hn, kv_w[:, 0:D]), w['k_norm_row'][...])
    v_ref[...] = _dot(hn, kv_w[:, D:2 * D])
    lfe = _log_sigmoid(_dot(hn, kv_w[:, 2 * D:2 * D + LANES]) + w['fbias_row'][...])
    lf_ref[...] = lfe[:, 0:N_HEADS]
    q, gate = _fox_proj(h1, w)
    q_ref[...] = q
    gate_ref[...] = gate


def _pre_prompt(x, weights):
    B, S, D = x.shape
    T = PRE_T
    tok = lambda d: pl.BlockSpec((1, T, d), lambda b, j: (b, j, 0))
    feat = lambda n: pl.BlockSpec((1, n, T), lambda b, j: (b, 0, j))
    out_shape = [
        jax.ShapeDtypeStruct((B, S, D), F32),
        jax.ShapeDtypeStruct((B, D, S), F32),
        jax.ShapeDtypeStruct((B, D, S), F32),
        jax.ShapeDtypeStruct((B, N_HEADS, S), F32),
        jax.ShapeDtypeStruct((B, D, S), BF16),
        jax.ShapeDtypeStruct((B, LANES, S), BF16),
        jax.ShapeDtypeStruct((B, LANES, S), BF16),
        jax.ShapeDtypeStruct((B, D, S), BF16),
        jax.ShapeDtypeStruct((B, SUBLANES, D), F32),
    ]
    out_specs = [tok(D), feat(D), feat(D), feat(N_HEADS), feat(D), feat(LANES), feat(LANES), feat(D),
                 pl.BlockSpec((1, SUBLANES, D), lambda b, j: (b, 0, 0))]
    const = lambda b, j: (0, 0)
    w_specs = [pl.BlockSpec(w.shape, const, pipeline_mode=pl.Buffered(1)) for w in weights]
    return pl.pallas_call(
        _pre_prompt_kernel,
        grid=(B, S // T),
        in_specs=[tok(D)] + w_specs,
        out_specs=out_specs,
        out_shape=out_shape,
        scratch_shapes=[pltpu.VMEM((SUBLANES, D), F32), pltpu.VMEM((LANES, LANES), F32)],
        compiler_params=pltpu.CompilerParams(
            dimension_semantics=("arbitrary", "arbitrary"), vmem_limit_bytes=VMEM_LIMIT),
        name="pre_prompt",
    )(x, *weights)


def _pre_sample(x2, p1, p2, weights):
    R, D = x2.shape
    out_shape = [
        jax.ShapeDtypeStruct((R, D), F32),
        jax.ShapeDtypeStruct((R, D), F32),
        jax.ShapeDtypeStruct((R, D), F32),
        jax.ShapeDtypeStruct((R, N_HEADS), F32),
        jax.ShapeDtypeStruct((R, D), BF16),
        jax.ShapeDtypeStruct((R, D), BF16),
        jax.ShapeDtypeStruct((R, D), F32),
    ]
    return pl.pallas_call(
        _pre_sample_kernel,
        out_shape=out_shape,
        compiler_params=pltpu.CompilerParams(vmem_limit_bytes=VMEM_LIMIT),
        name="pre_sample",
    )(x2, p1, p2, *weights)


def _attn_kernel(qT_ref, qfT_ref, kT_ref, kfT_ref, vT_ref, gateT_ref, oT_ref, kbuf, vbuf):
    T = qT_ref.shape[2]
    hp = pl.program_id(1)
    i = pl.program_id(2)

    @pl.when(i == 0)
    def _():
        for kt in range(kbuf.shape[0]):
            cols = slice(kt * T, (kt + 1) * T)
            kbuf[kt, :, 0:LANES] = kT_ref[0, :, cols].T.astype(BF16)
            kbuf[kt, :, LANES:2 * LANES] = kfT_ref[0, :, cols].astype(F32).T.astype(BF16)
            vbuf[kt] = vT_ref[0, :, cols].astype(BF16)

    row = lax.broadcasted_iota(jnp.int32, (LANES, 1), 0)
    lo_half = row < HEAD_DIM
    q2 = qT_ref[0]
    qf = qfT_ref[0]
    zero = jnp.zeros_like(q2)
    rhs = jnp.concatenate([
        jnp.concatenate([jnp.where(lo_half, q2, zero), jnp.where(row // AUG == 2 * hp, qf, zero)], axis=0),
        jnp.concatenate([jnp.where(lo_half, zero, q2), jnp.where(row // AUG == 2 * hp + 1, qf, zero)], axis=0),
    ], axis=1)

    def attend(n_tiles):
        ss = [_dot(kbuf[j], rhs) for j in range(n_tiles)]
        r = lax.broadcasted_iota(jnp.int32, ss[-1].shape, 0)
        c = lax.broadcasted_iota(jnp.int32, ss[-1].shape, 1)
        ss[-1] = jnp.where(r <= jnp.where(c >= T, c - T, c), ss[-1], NEG_BIG)
        m = None
        for s in ss:
            ms = jnp.max(s, axis=0, keepdims=True)
            m = ms if m is None else jnp.maximum(m, ms)
        acc = l = None
        for j, s in enumerate(ss):
            p = jnp.exp2(s - m)
            ps = jnp.sum(p, axis=0, keepdims=True)
            l = ps if l is None else l + ps
            d = _dot(vbuf[j], p.astype(BF16))
            acc = d if acc is None else acc + d
        o = acc / l
        o = jnp.where(lo_half, o[:, 0:T], o[:, T:2 * T])
        oT_ref[0] = (o * gateT_ref[0].astype(F32)).astype(BF16)

    for v in range(kbuf.shape[0]):
        @pl.when(i == v)
        def _(v=v):
            attend(v + 1)


def _attention(qT, qfT, kT, kfT, vT, gateT):
    B, D, S = qT.shape
    T = ATT_T
    qspec = pl.BlockSpec((1, LANES, T), lambda b, hp, i: (b, hp, i))
    kspec = pl.BlockSpec((1, LANES, S), lambda b, hp, i: (b, hp, 0))
    return pl.pallas_call(
        _attn_kernel,
        grid=(B, D // LANES, S // T),
        in_specs=[qspec, pl.BlockSpec((1, LANES, T), lambda b, hp, i: (b, 0, i)),
                  kspec, pl.BlockSpec((1, LANES, S), lambda b, hp, i: (b, 0, 0)), kspec, qspec],
        out_specs=qspec,
        out_shape=jax.ShapeDtypeStruct((B, D, S), BF16),
        scratch_shapes=[pltpu.VMEM((S // T, T, 2 * LANES), BF16), pltpu.VMEM((S // T, LANES, T), BF16)],
        compiler_params=pltpu.CompilerParams(
            dimension_semantics=("arbitrary", "arbitrary", "arbitrary"), vmem_limit_bytes=VMEM_LIMIT),
        name="prompt_attn",
    )(qT, qfT, kT, kfT, vT, gateT)


def _sattn_kernel(pt_ref, *refs):
    P = PAGES_PER_STEP
    kpages = refs[0:P]
    vpages = refs[P:2 * P]
    lpages = refs[2 * P:3 * P]
    (knew_ref, vnew_ref, lnew_ref, q_ref, gate_ref, tri_ref,
     o_ref, qblk, gcarry, m_scr, l_scr, acc_scr) = refs[3 * P:]
    c = pl.program_id(1)
    nq = q_ref.shape[1]
    R = LANES
    PG = LANES
    D = D_MODEL

    @pl.when(c == 0)
    def _():
        q = q_ref[0]
        rows = lax.broadcasted_iota(jnp.int32, (R, D), 0)
        cols = lax.broadcasted_iota(jnp.int32, (R, D), 1)
        tiled = jnp.concatenate([q] * (R // nq), axis=0)
        qblk[...] = jnp.where(cols // HEAD_DIM == rows // nq, tiled, jnp.zeros_like(tiled))
        gcarry[...] = jnp.zeros_like(gcarry)
        m_scr[...] = jnp.full_like(m_scr, NEG_BIG)
        l_scr[...] = jnp.zeros_like(l_scr)
        acc_scr[...] = jnp.zeros_like(acc_scr)

    tri_u = tri_ref[...]

    def bias_rows(lf_pages):
        cs = _dot_exact_lhs(jnp.concatenate(lf_pages, axis=0), tri_u)
        carry = gcarry[...]
        out = []
        for j in range(len(lf_pages)):
            G = cs[j * N_HEADS:(j + 1) * N_HEADS] + carry
            carry = jnp.broadcast_to(G[:, PG - 1:PG], carry.shape)
            out.append(jnp.concatenate(
                [jnp.broadcast_to(G[h:h + 1], (nq, PG)) for h in range(N_HEADS)], axis=0))
        gcarry[...] = carry
        return jnp.concatenate(out, axis=1) if len(out) > 1 else out[0]

    def update(s, vT):
        m_prev = m_scr[...]
        m_new = jnp.maximum(m_prev, jnp.max(s, axis=1, keepdims=True))
        alpha = jnp.exp2(m_prev - m_new)
        p = jnp.exp2(s - pltpu.repeat(m_new, s.shape[1] // LANES, axis=1))
        l_scr[...] = alpha * l_scr[...] + jnp.sum(p, axis=1, keepdims=True)
        acc_scr[...] = jnp.concatenate([alpha.T] * (D // LANES), axis=0) * acc_scr[...] + _dot_nt(vT, p.astype(BF16))
        m_scr[...] = m_new

    kT = jnp.concatenate([kpages[j][0].astype(BF16) for j in range(P)], axis=1)
    vT = jnp.concatenate([vpages[j][0].astype(BF16) for j in range(P)], axis=1)
    s = _dot(qblk[...], kT) - bias_rows([lpages[j][0] for j in range(P)]) * LOG2E
    update(s, vT)

    @pl.when(c == pl.num_programs(1) - 1)
    def _():
        s = _dot(qblk[...], knew_ref[0].astype(BF16)) - bias_rows([lnew_ref[0]]) * LOG2E
        key = lax.broadcasted_iota(jnp.int32, s.shape, 1)
        qi = lax.broadcasted_iota(jnp.int32, s.shape, 0) % nq
        update(jnp.where(key <= qi, s, NEG_BIG), vnew_ref[0].astype(BF16))

        linv = 1.0 / l_scr[...]
        oT = acc_scr[...] * jnp.concatenate([linv.T] * (D // LANES), axis=0)
        rows = lax.broadcasted_iota(jnp.int32, oT.shape, 0)
        cols = lax.broadcasted_iota(jnp.int32, oT.shape, 1)
        oT = jnp.where(rows // HEAD_DIM == cols // nq, oT, 0.0)
        shift = nq
        while shift < LANES:
            oT = oT + pltpu.roll(oT, shift, axis=1)
            shift *= 2
        o = oT.T[0:nq]
        o_ref[0] = (o * gate_ref[0].astype(F32)).astype(BF16)


def _sample_attention(page_table, ckT, cvT, clT, knewT, vnewT, lnewT, q, gate, tri_u):
    NB, NQ, D = q.shape
    n_pages = page_table.shape[1]
    P = PAGES_PER_STEP
    PG = ckT.shape[2]

    def page_spec(rows, j):
        return pl.BlockSpec((1, rows, PG), lambda b, c, pt: (pt[b, c * P + j], 0, 0))

    per_seq = lambda rows, width: pl.BlockSpec((1, rows, width), lambda b, c, pt: (b, 0, 0))
    in_specs = ([page_spec(D, j) for j in range(P)] + [page_spec(D, j) for j in range(P)]
                + [page_spec(N_HEADS, j) for j in range(P)]
                + [per_seq(D, PG), per_seq(D, PG), per_seq(N_HEADS, PG), per_seq(NQ, D), per_seq(NQ, D),
                   pl.BlockSpec(tri_u.shape, lambda b, c, pt: (0, 0))])
    grid_spec = pltpu.PrefetchScalarGridSpec(
        num_scalar_prefetch=1,
        grid=(NB, n_pages // P),
        in_specs=in_specs,
        out_specs=per_seq(NQ, D),
        scratch_shapes=[pltpu.VMEM((LANES, D), BF16),
                        pltpu.VMEM((N_HEADS, LANES), F32),
                        pltpu.VMEM((LANES, LANES), F32), pltpu.VMEM((LANES, LANES), F32),
                        pltpu.VMEM((D, LANES), F32)],
    )
    return pl.pallas_call(
        _sattn_kernel,
        grid_spec=grid_spec,
        out_shape=jax.ShapeDtypeStruct((NB, NQ, D), BF16),
        compiler_params=pltpu.CompilerParams(
            dimension_semantics=("arbitrary", "arbitrary"), vmem_limit_bytes=VMEM_LIMIT),
        name="sample_attn",
    )(page_table, *([ckT] * P), *([cvT] * P), *([clT] * P), knewT, vnewT, lnewT, q, gate, tri_u)


def _outproj_fm_kernel(ogT_ref, h_ref, w_ref, y_ref):
    y_ref[0] = h_ref[0] + lax.dot_general(ogT_ref[0], w_ref[...], (((0,), (0,)), ((), ())),
                                          preferred_element_type=F32)


def _outproj_fm(ogT, h, w):
    B, D, S = ogT.shape
    T = OUT_T
    return pl.pallas_call(
        _outproj_fm_kernel,
        grid=(B, S // T),
        in_specs=[pl.BlockSpec((1, D, T), lambda b, j: (b, 0, j)), pl.BlockSpec((1, T, D), lambda b, j: (b, j, 0)),
                  pl.BlockSpec(w.shape, lambda b, j: (0, 0), pipeline_mode=pl.Buffered(1))],
        out_specs=pl.BlockSpec((1, T, D), lambda b, j: (b, j, 0)),
        out_shape=jax.ShapeDtypeStruct((B, S, D), F32),
        compiler_params=pltpu.CompilerParams(
            dimension_semantics=("arbitrary", "arbitrary"), vmem_limit_bytes=VMEM_LIMIT),
        name="outproj_prompt",
    )(ogT, h, w)


def _outproj_kernel(og_ref, h_ref, w_ref, y_ref):
    y_ref[...] = h_ref[...] + _dot(og_ref[...], w_ref[...])


def _outproj(og, h, w):
    R, D = h.shape
    T = min(OUT_T, R)
    row = pl.BlockSpec((T, D), lambda i: (i, 0))
    return pl.pallas_call(
        _outproj_kernel,
        grid=(R // T,),
        in_specs=[row, row, pl.BlockSpec(w.shape, lambda i: (0, 0), pipeline_mode=pl.Buffered(1))],
        out_specs=row,
        out_shape=jax.ShapeDtypeStruct((R, D), F32),
        compiler_params=pltpu.CompilerParams(
            dimension_semantics=("arbitrary",), vmem_limit_bytes=VMEM_LIMIT),
        name="outproj",
    )(og, h, w)


def _upper_tri(n):
    return (jnp.arange(n)[:, None] <= jnp.arange(n)[None, :]).astype(BF16)


def kernel(x_prompt, x_sample, state_conv, cache_k, cache_v, cache_logf, page_table,
           a_norm, a_w_in, a_conv, a_w_out, kv_norm, kv_w, kv_fbias, k_norm,
           b_norm, b_w_in, q_norm, b_w_out):
    B, S, D = x_prompt.shape
    NB, NQ, _ = x_sample.shape
    n_pool, PG = cache_k.shape[0], cache_k.shape[1]
    H, hd = N_HEADS, HEAD_DIM
    assert a_norm.shape[0] == 1 and b_norm.shape[0] == 1, "one conv layer then one attention layer"
    assert D == D_MODEL and NQ == SUBLANES and PG == LANES

    common = [a_norm[0][None], a_w_in[0].astype(BF16), a_conv[0], a_w_out[0].astype(BF16),
              kv_norm[None], b_norm[0][None]]
    b_w_in_b = b_w_in[0].astype(BF16)
    kv_wT = kv_w.T
    kv_wT_ext = jnp.concatenate([kv_wT[:2 * D], jnp.repeat(kv_wT[2 * D:], AUG, axis=0), kv_wT[2 * D:]], axis=0)
    fbias_col = jnp.broadcast_to(jnp.concatenate([jnp.repeat(kv_fbias, AUG), kv_fbias])[:, None],
                                 (LANES + H, LANES))
    k_norm_col = jnp.broadcast_to(jnp.tile(k_norm, H)[:, None], (D, LANES))
    q_norm_col = jnp.broadcast_to(jnp.tile(q_norm[0], H)[:, None], (D, LANES))
    w_prompt = common + [kv_wT_ext.astype(BF16), fbias_col, k_norm_col, _upper_tri(PRE_T), b_w_in_b.T, q_norm_col]
    kv_w_row = jnp.concatenate([kv_w, jnp.zeros((D, LANES - H), kv_w.dtype)], axis=1)
    fbias_row = jnp.concatenate([kv_fbias, jnp.zeros((LANES - H,), kv_fbias.dtype)])[None]
    w_sample = common + [kv_w_row.astype(BF16), fbias_row, jnp.tile(k_norm, 2)[None], b_w_in_b,
                         jnp.tile(q_norm[0], 2)[None]]
    w_out_b = b_w_out[0].astype(BF16)

    (h1, kT, vT, lfT, qT, qfT, kfT, gateT, tail) = _pre_prompt(x_prompt, w_prompt)
    ogT = _attention(qT, qfT, kT, kfT, vT, gateT)
    y_prompt = _outproj_fm(ogT, h1, w_out_b)

    st = state_conv[0]
    p1 = jnp.pad(st[:, 1:2], ((0, 0), (0, NQ - 1), (0, 0))).reshape(NB * NQ, D)
    p2 = jnp.pad(st, ((0, 0), (0, NQ - 2), (0, 0))).reshape(NB * NQ, D)
    (h1s, k_s, v_s, lf_s, qs, gate_s, cv_s) = _pre_sample(x_sample.reshape(NB * NQ, D), p1, p2, w_sample)
    new_page = lambda a: jnp.pad(a.reshape(NB, NQ, -1), ((0, 0), (0, PG - NQ), (0, 0))).transpose(0, 2, 1)
    ogs = _sample_attention(
        page_table,
        cache_k.transpose(0, 2, 3, 1).reshape(n_pool, D, PG),
        cache_v.transpose(0, 2, 3, 1).reshape(n_pool, D, PG),
        cache_logf.transpose(0, 2, 1),
        new_page(k_s), new_page(v_s), new_page(lf_s),
        qs.reshape(NB, NQ, D), gate_s.reshape(NB, NQ, D), _upper_tri(PG))
    y_sample = _outproj(ogs.reshape(NB * NQ, D), h1s, w_out_b).reshape(NB, NQ, D)

    conv_prompt = tail[:, SUBLANES - (CONV_W - 1):][None]
    conv_sample = cv_s.reshape(NB, NQ, D)[:, NQ - (CONV_W - 1):][None]
    to_bshd = lambda t: t.reshape(B, H, hd, S).transpose(0, 3, 1, 2)
    return (y_prompt, y_sample, conv_prompt, conv_sample,
            to_bshd(kT), to_bshd(vT), lfT.transpose(0, 2, 1),
            k_s.reshape(NB, NQ, H, hd), v_s.reshape(NB, NQ, H, hd), lf_s.reshape(NB, NQ, H))
```

```python
import jax
import jax.numpy as jnp
from jax import lax
from jax.experimental import pallas as pl
from jax.experimental.pallas import tpu as pltpu

F32 = jnp.float32
BF16 = jnp.bfloat16

D_MODEL = 1024
N_HEADS = 16
HEAD_DIM = 64
CONV_W = 3
EPS = 1e-6
LOG2E = 1.4426950408889634
NEG_BIG = -1e30

LANES = 128
SUBLANES = 8
AUG = LANES // N_HEADS
VMEM_LIMIT = 56 * 1024 * 1024

PRE_T = 256
ATT_T = 256
ATT_WAVE = 3
OUT_T = 512
PAGES_PER_STEP = 8
SEQS_PER_BATCH_STEP = 4


def _dot(a, b):
    return jnp.dot(a, b, preferred_element_type=F32)


def _dot_nt(a, b):
    return lax.dot_general(a, b, (((1,), (1,)), ((), ())), preferred_element_type=F32)


def _tile_lanes(x, n):
    return jnp.concatenate([x] * n, axis=1) if n > 1 else x


def _split3(x):
    hi = x.astype(BF16)
    r1 = x - hi.astype(F32)
    mid = r1.astype(BF16)
    r2 = r1 - mid.astype(F32)
    return hi, mid, r2.astype(BF16)


def _dot_exact_rhs(m_bf16, x_f32):
    hi, mid, lo = _split3(x_f32)
    return _dot(m_bf16, hi) + _dot(m_bf16, mid) + _dot(m_bf16, lo)


def _dot_exact_lhs(x_f32, m_bf16):
    hi, mid, lo = _split3(x_f32)
    return _dot(hi, m_bf16) + _dot(mid, m_bf16) + _dot(lo, m_bf16)


def _rms(x, g):
    r = lax.rsqrt(jnp.mean(x * x, axis=-1, keepdims=True) + EPS)
    return x * r * g


def _head_rms(x, g128):
    lane = lax.broadcasted_iota(jnp.int32, (1, LANES), 1)
    lo_half = lane < HEAD_DIM
    outs = []
    for p in range(x.shape[1] // LANES):
        xp = x[:, p * LANES:(p + 1) * LANES]
        sq = xp * xp
        s_lo = jnp.sum(jnp.where(lo_half, sq, 0.0), axis=-1, keepdims=True)
        s_hi = jnp.sum(jnp.where(lo_half, 0.0, sq), axis=-1, keepdims=True)
        r = jnp.where(lo_half, lax.rsqrt(s_lo / HEAD_DIM + EPS), lax.rsqrt(s_hi / HEAD_DIM + EPS))
        outs.append(xp * r * g128)
    return jnp.concatenate(outs, axis=1)


def _head_rms_fm(xT, gT):
    n, T = xT.shape
    x3 = xT.reshape(n // HEAD_DIM, HEAD_DIM, T)
    r = lax.rsqrt(jnp.mean(x3 * x3, axis=1, keepdims=True) + EPS)
    return (x3 * r).reshape(n, T) * gT


def _silu(z):
    return z * jax.nn.sigmoid(z)


def _log_sigmoid(x):
    return jnp.minimum(x, 0.0) - jnp.log1p(jnp.exp(-jnp.abs(x)))


def _bias_select(F, j, key_side):
    hi, mid, lo = (t.astype(F32) for t in _split3(F))
    if key_side:
        out = jnp.where(j < 3, 1.0, jnp.where(j == 3, -hi, jnp.where(j == 4, -mid, jnp.where(j == 5, -lo, 0.0))))
    else:
        out = jnp.where(j == 0, hi, jnp.where(j == 1, mid, jnp.where(j == 2, lo, jnp.where(j < 6, 1.0, 0.0))))
    return out.astype(BF16)


def _conv_layer(x, prev1, prev2, m1, m2, w):
    D = D_MODEL
    xn = _rms(x, w['a_norm'][...]).astype(BF16)
    w_in = w['a_w_in']
    v = _dot(xn, w_in[:, D:2 * D]) * _dot(xn, w_in[:, 2 * D:3 * D])
    v1 = jnp.where(m1, prev1, pltpu.roll(v, 1, axis=0))
    v2 = jnp.where(m2, prev2, pltpu.roll(v, 2, axis=0))
    cw = w['a_conv'][...]
    conv = v2 * cw[0:1] + v1 * cw[1:2] + v * cw[2:3]
    b = _dot(xn, w_in[:, 0:D])
    z = _dot(xn, w_in[:, 3 * D:4 * D])
    g = (_silu(z) * b * conv).astype(BF16)
    return v, x + _dot(g, w['a_w_out'][...])


def _fox_proj(h1, w):
    D = D_MODEL
    xn = _rms(h1, w['b_norm'][...]).astype(BF16)
    b_w_in = w['b_w_in']
    q = _head_rms(_dot(xn, b_w_in[:, 0:D]), w['q_norm'][...]) * (LOG2E * HEAD_DIM ** -0.5)
    gate = _silu(_dot(xn, b_w_in[:, D:2 * D]))
    return q.astype(BF16), gate.astype(BF16)


_W_COMMON = ('a_norm', 'a_w_in', 'a_conv', 'a_w_out', 'kv_norm', 'b_norm')
_W_PROMPT = _W_COMMON + ('kv_wT', 'fbias_col', 'k_norm_col', 'tri_u', 'b_w_inT', 'q_norm_col')
_W_SAMPLE = _W_COMMON + ('kv_w', 'fbias_row', 'k_norm_row', 'b_w_in', 'q_norm')


def _pre_prompt_kernel(x_ref, *refs):
    nw = len(_W_PROMPT)
    w = dict(zip(_W_PROMPT, refs[:nw]))
    (h1_ref, kT_ref, vT_ref, lfT_ref, qT_ref, qfT_ref, kfT_ref, gateT_ref, tail_ref, vcarry, fcarry) = refs[nw:]
    T = x_ref.shape[1]
    D = D_MODEL
    reps = T // LANES

    @pl.when(pl.program_id(1) == 0)
    def _():
        vcarry[...] = jnp.zeros_like(vcarry)
        fcarry[...] = jnp.zeros_like(fcarry)

    row = lax.broadcasted_iota(jnp.int32, (T, 1), 0)
    prev = vcarry[...]
    pm1 = prev[SUBLANES - 1:SUBLANES]
    pm2 = prev[SUBLANES - 2:SUBLANES - 1]
    v, h1 = _conv_layer(x_ref[0], pm1, jnp.where(row == 0, pm2, pm1), row == 0, row < 2, w)
    tail = v[T - SUBLANES:T]
    vcarry[...] = tail
    tail_ref[0] = tail
    h1_ref[0] = h1

    hn = _rms(h1, w['kv_norm'][...]).astype(BF16)
    kvT = w['kv_wT']
    kT_ref[0] = _head_rms_fm(_dot_nt(kvT[0:D, :], hn), _tile_lanes(w['k_norm_col'][...], reps))
    vT_ref[0] = _dot_nt(kvT[D:2 * D, :], hn)
    fT = _dot_nt(kvT[2 * D:2 * D + LANES + N_HEADS, :], hn) + _tile_lanes(w['fbias_col'][...], reps)
    lfT = _log_sigmoid(fT)
    lfT_ref[0] = lfT[LANES:LANES + N_HEADS]
    FT = _dot_exact_lhs(lfT[0:LANES], w['tri_u'][...]) + _tile_lanes(fcarry[...], reps)
    fcarry[...] = jnp.broadcast_to(FT[:, T - 1:T], fcarry.shape)
    Fp = FT * LOG2E
    slot = lax.broadcasted_iota(jnp.int32, (LANES, 1), 0) % AUG
    kfT_ref[0] = _bias_select(Fp, slot, True)
    qfT_ref[0] = _bias_select(Fp, slot, False)

    xn2 = _rms(h1, w['b_norm'][...]).astype(BF16)
    b_w_inT = w['b_w_inT']
    qT = _head_rms_fm(_dot_nt(b_w_inT[0:D, :], xn2), _tile_lanes(w['q_norm_col'][...], reps))
    qT_ref[0] = (qT * (LOG2E * HEAD_DIM ** -0.5)).astype(BF16)
    gateT_ref[0] = _silu(_dot_nt(b_w_inT[D:2 * D, :], xn2)).astype(BF16)


def _pre_sample_kernel(x_ref, p1_ref, p2_ref, *refs):
    nw = len(_W_SAMPLE)
    w = dict(zip(_W_SAMPLE, refs[:nw]))
    h1_ref, k_ref, v_ref, lf_ref, q_ref, gate_ref, cv_ref = refs[nw:]
    D = D_MODEL
    t = lax.broadcasted_iota(jnp.int32, (x_ref.shape[0], 1), 0) % SUBLANES
    v, h1 = _conv_layer(x_ref[...], p1_ref[...], p2_ref[...], t == 0, t < 2, w)
    cv_ref[...] = v
    h1_ref[...] = h1
    hn = _rms(h1, w['kv_norm'][...]).astype(BF16)
    kv_w = w['kv_w']
    k_ref[...] = _head_rms(_dot(hn, kv_w[:, 0:D]), w['k_norm_row'][...])
    v_ref[...] = _dot(hn, kv_w[:, D:2 * D])
    lfe = _log_sigmoid(_dot(hn, kv_w[:, 2 * D:2 * D + LANES]) + w['fbias_row'][...])
    lf_ref[...] = lfe[:, 0:N_HEADS]
    q, gate = _fox_proj(h1, w)
    q_ref[...] = q
    gate_ref[...] = gate


def _pre_prompt(x, weights):
    B, S, D = x.shape
    T = PRE_T
    tok = lambda d: pl.BlockSpec((1, T, d), lambda b, j: (b, j, 0))
    feat = lambda n: pl.BlockSpec((1, n, T), lambda b, j: (b, 0, j))
    out_shape = [
        jax.ShapeDtypeStruct((B, S, D), F32),
        jax.ShapeDtypeStruct((B, D, S), F32),
        jax.ShapeDtypeStruct((B, D, S), F32),
        jax.ShapeDtypeStruct((B, N_HEADS, S), F32),
        jax.ShapeDtypeStruct((B, D, S), BF16),
        jax.ShapeDtypeStruct((B, LANES, S), BF16),
        jax.ShapeDtypeStruct((B, LANES, S), BF16),
        jax.ShapeDtypeStruct((B, D, S), BF16),
        jax.ShapeDtypeStruct((B, SUBLANES, D), F32),
    ]
    out_specs = [tok(D), feat(D), feat(D), feat(N_HEADS), feat(D), feat(LANES), feat(LANES), feat(D),
                 pl.BlockSpec((1, SUBLANES, D), lambda b, j: (b, 0, 0))]
    const = lambda b, j: (0, 0)
    w_specs = [pl.BlockSpec(w.shape, const, pipeline_mode=pl.Buffered(1)) for w in weights]
    return pl.pallas_call(
        _pre_prompt_kernel,
        grid=(B, S // T),
        in_specs=[tok(D)] + w_specs,
        out_specs=out_specs,
        out_shape=out_shape,
        scratch_shapes=[pltpu.VMEM((SUBLANES, D), F32), pltpu.VMEM((LANES, LANES), F32)],
        compiler_params=pltpu.CompilerParams(
            dimension_semantics=("arbitrary", "arbitrary"), vmem_limit_bytes=VMEM_LIMIT),
        name="pre_prompt",
    )(x, *weights)


def _pre_sample(x2, p1, p2, weights):
    R, D = x2.shape
    out_shape = [
        jax.ShapeDtypeStruct((R, D), F32),
        jax.ShapeDtypeStruct((R, D), F32),
        jax.ShapeDtypeStruct((R, D), F32),
        jax.ShapeDtypeStruct((R, N_HEADS), F32),
        jax.ShapeDtypeStruct((R, D), BF16),
        jax.ShapeDtypeStruct((R, D), BF16),
        jax.ShapeDtypeStruct((R, D), F32),
    ]
    return pl.pallas_call(
        _pre_sample_kernel,
        out_shape=out_shape,
        compiler_params=pltpu.CompilerParams(vmem_limit_bytes=VMEM_LIMIT),
        name="pre_sample",
    )(x2, p1, p2, *weights)


def _prompt_attn_step(hp, j, qa_ref, qb_ref, qfa_ref, qfb_ref, kT_ref, kfT_ref, vT_ref, ga_ref, gb_ref,
                      o_ref, kbuf, vbuf):
    T = qa_ref.shape[2]
    n = kbuf.shape[0]

    @pl.when(j == 0)
    def _():
        for kt in range(n):
            cols = slice(kt * T, (kt + 1) * T)
            kbuf[kt, :, 0:LANES] = kT_ref[0, :, cols].T.astype(BF16)
            kbuf[kt, :, LANES:2 * LANES] = kfT_ref[0, :, cols].astype(F32).T.astype(BF16)
            vbuf[kt] = vT_ref[0, :, cols].astype(BF16)

    row = lax.broadcasted_iota(jnp.int32, (LANES, 1), 0)
    lo_half = row < HEAD_DIM

    def query_operand(q_ref, qf_ref):
        q2 = q_ref[0]
        qf = qf_ref[0]
        zero = jnp.zeros_like(q2)
        return jnp.concatenate([
            jnp.concatenate([jnp.where(lo_half, q2, zero), jnp.where(row // AUG == 2 * hp, qf, zero)], axis=0),
            jnp.concatenate([jnp.where(lo_half, zero, q2), jnp.where(row // AUG == 2 * hp + 1, qf, zero)], axis=0),
        ], axis=1)

    def scores(rhs, tiles, n_tiles):
        ss = [_dot(kbuf[kt], rhs) for kt in tiles]
        if tiles[-1] == n_tiles - 1:
            r = lax.broadcasted_iota(jnp.int32, ss[-1].shape, 0)
            c = lax.broadcasted_iota(jnp.int32, ss[-1].shape, 1)
            ss[-1] = jnp.where(r <= jnp.where(c >= T, c - T, c), ss[-1], NEG_BIG)
        m = None
        for s in ss:
            ms = jnp.max(s, axis=0, keepdims=True)
            m = ms if m is None else jnp.maximum(m, ms)
        return ss, m

    def run(problems):
        waves = []
        for pi, pr in enumerate(problems):
            for c0 in range(0, pr[3], ATT_WAVE):
                waves.append((pi, list(range(c0, min(c0 + ATT_WAVE, pr[3])))))
        rhs = [query_operand(pr[0], pr[1]) for pr in problems]
        state = [None] * len(problems)
        cur = scores(rhs[waves[0][0]], waves[0][1], problems[waves[0][0]][3])
        for w, (pi, tiles) in enumerate(waves):
            nxt = None
            if w + 1 < len(waves):
                npi, ntiles = waves[w + 1]
                nxt = scores(rhs[npi], ntiles, problems[npi][3])
            ss, m_w = cur
            m_new = m_w if state[pi] is None else jnp.maximum(state[pi][0], m_w)
            acc_w = l_w = None
            for kt, s in zip(tiles, ss):
                p = jnp.exp2(s - m_new)
                ps = jnp.sum(p, axis=0, keepdims=True)
                l_w = ps if l_w is None else l_w + ps
                v = vbuf[kt]
                if nxt is not None:
                    keep = (pltpu.bitcast(nxt[1][:, 0:T], jnp.int32) | 1) != 0
                    v = jnp.where(keep, v, jnp.zeros_like(v))
                d = _dot(v, p.astype(BF16))
                acc_w = d if acc_w is None else acc_w + d
            if state[pi] is None:
                state[pi] = (m_new, l_w, acc_w)
            else:
                alpha = jnp.exp2(state[pi][0] - m_new)
                state[pi] = (m_new, alpha * state[pi][1] + l_w, alpha * state[pi][2] + acc_w)
            cur = nxt
        for pr, (_, l, acc) in zip(problems, state):
            o = acc / l
            o = jnp.where(lo_half, o[:, 0:T], o[:, T:2 * T])
            o_ref[pr[4], 0] = (o * pr[2][0].astype(F32)).astype(BF16)

    for v in range(n // 2):
        @pl.when(j == v)
        def _(v=v):
            run([(qa_ref, qfa_ref, ga_ref, v + 1, 0), (qb_ref, qfb_ref, gb_ref, n - v, 1)])


def _sample_attn_step(c, n_chunks, refs):
    P = PAGES_PER_STEP
    kpages = refs[0:P]
    vpages = refs[P:2 * P]
    lpages = refs[2 * P:3 * P]
    (knew_ref, vnew_ref, lnew_ref, q_ref, gate_ref, tri_ref,
     o_ref, qblk, gcarry, m_scr, l_scr, acc_scr) = refs[3 * P:]
    nq = q_ref.shape[1]
    R = LANES
    PG = LANES
    D = D_MODEL

    @pl.when(c == 0)
    def _():
        q = q_ref[0]
        rows = lax.broadcasted_iota(jnp.int32, (R, D), 0)
        cols = lax.broadcasted_iota(jnp.int32, (R, D), 1)
        tiled = jnp.concatenate([q] * (R // nq), axis=0)
        qblk[...] = jnp.where(cols // HEAD_DIM == rows // nq, tiled, jnp.zeros_like(tiled))
        gcarry[...] = jnp.zeros_like(gcarry)
        m_scr[...] = jnp.full_like(m_scr, NEG_BIG)
        l_scr[...] = jnp.zeros_like(l_scr)
        acc_scr[...] = jnp.zeros_like(acc_scr)

    tri_u = tri_ref[...]

    def bias_rows(lf_pages):
        cs = _dot_exact_lhs(jnp.concatenate(lf_pages, axis=0), tri_u)
        carry = gcarry[...]
        out = []
        for j in range(len(lf_pages)):
            G = cs[j * N_HEADS:(j + 1) * N_HEADS] + carry
            carry = jnp.broadcast_to(G[:, PG - 1:PG], carry.shape)
            out.append(jnp.concatenate(
                [jnp.broadcast_to(G[h:h + 1], (nq, PG)) for h in range(N_HEADS)], axis=0))
        gcarry[...] = carry
        return jnp.concatenate(out, axis=1) if len(out) > 1 else out[0]

    def update(s, vT):
        m_prev = m_scr[...]
        m_new = jnp.maximum(m_prev, jnp.max(s, axis=1, keepdims=True))
        alpha = jnp.exp2(m_prev - m_new)
        p = jnp.exp2(s - _tile_lanes(m_new, s.shape[1] // LANES))
        l_scr[...] = alpha * l_scr[...] + jnp.sum(p, axis=1, keepdims=True)
        acc_scr[...] = jnp.concatenate([alpha.T] * (D // LANES), axis=0) * acc_scr[...] + _dot_nt(vT, p.astype(BF16))
        m_scr[...] = m_new

    kT = jnp.concatenate([kpages[j][0].astype(BF16) for j in range(P)], axis=1)
    vT = jnp.concatenate([vpages[j][0].astype(BF16) for j in range(P)], axis=1)
    s = _dot(qblk[...], kT) - bias_rows([lpages[j][0] for j in range(P)]) * LOG2E
    update(s, vT)

    @pl.when(c == n_chunks - 1)
    def _():
        s = _dot(qblk[...], knew_ref[0].astype(BF16)) - bias_rows([lnew_ref[0]]) * LOG2E
        key = lax.broadcasted_iota(jnp.int32, s.shape, 1)
        qi = lax.broadcasted_iota(jnp.int32, s.shape, 0) % nq
        update(jnp.where(key <= qi, s, NEG_BIG), vnew_ref[0].astype(BF16))

        linv = 1.0 / l_scr[...]
        oT = acc_scr[...] * jnp.concatenate([linv.T] * (D // LANES), axis=0)
        rows = lax.broadcasted_iota(jnp.int32, oT.shape, 0)
        cols = lax.broadcasted_iota(jnp.int32, oT.shape, 1)
        oT = jnp.where(rows // HEAD_DIM == cols // nq, oT, 0.0)
        shift = nq
        while shift < LANES:
            oT = oT + pltpu.roll(oT, shift, axis=1)
            shift *= 2
        o = oT.T[0:nq]
        o_ref[0] = (o * gate_ref[0].astype(F32)).astype(BF16)


N_PROMPT_IN = 9


def _attn_kernel(pt_ref, *refs):
    del pt_ref
    n_sample_in = 3 * PAGES_PER_STEP + 6
    prompt_in = refs[:N_PROMPT_IN]
    sample_in = refs[N_PROMPT_IN:N_PROMPT_IN + n_sample_in]
    o_ref, os_ref, kbuf, vbuf, qblk, gcarry, m_scr, l_scr, acc_scr = refs[N_PROMPT_IN + n_sample_in:]
    hp = pl.program_id(1)
    j = pl.program_id(2)
    nj = kbuf.shape[0] // 2
    n_chunks = (D_MODEL // LANES) * nj // SEQS_PER_BATCH_STEP
    c = (hp * nj + j) % n_chunks
    _sample_attn_step(c, n_chunks, sample_in + (os_ref, qblk, gcarry, m_scr, l_scr, acc_scr))
    _prompt_attn_step(hp, j, *prompt_in, o_ref, kbuf, vbuf)


def _attention(qT, qfT, kT, kfT, vT, gateT, page_table, ckT, cvT, clT, knewT, vnewT, lnewT, qs, gates, tri_u):
    B, D, S = qT.shape
    NB, NQ, _ = qs.shape
    T = ATT_T
    n = S // T
    HP, NJ = D // LANES, n // 2
    P = PAGES_PER_STEP
    PG = ckT.shape[2]
    n_chunks = page_table.shape[1] // P
    assert HP * NJ == SEQS_PER_BATCH_STEP * n_chunks and NB == B * SEQS_PER_BATCH_STEP

    lo = lambda rows: pl.BlockSpec((1, LANES, T), lambda b, hp, j, pt: (b, hp if rows else 0, j))
    hi = lambda rows: pl.BlockSpec((1, LANES, T), lambda b, hp, j, pt: (b, hp if rows else 0, n - 1 - j))
    kspec = pl.BlockSpec((1, LANES, S), lambda b, hp, j, pt: (b, hp, 0))
    prompt_specs = [lo(True), hi(True), lo(False), hi(False),
                    kspec, pl.BlockSpec((1, LANES, S), lambda b, hp, j, pt: (b, 0, 0)), kspec, lo(True), hi(True)]

    seq = lambda b, hp, j: b * SEQS_PER_BATCH_STEP + (hp * NJ + j) // n_chunks

    def page_spec(rows, jj):
        return pl.BlockSpec(
            (1, rows, PG),
            lambda b, hp, j, pt: (pt[seq(b, hp, j), ((hp * NJ + j) % n_chunks) * P + jj], 0, 0))

    per_seq = lambda rows, width: pl.BlockSpec((1, rows, width), lambda b, hp, j, pt: (seq(b, hp, j), 0, 0))
    sample_specs = ([page_spec(D, jj) for jj in range(P)] + [page_spec(D, jj) for jj in range(P)]
                    + [page_spec(N_HEADS, jj) for jj in range(P)]
                    + [per_seq(D, PG), per_seq(D, PG), per_seq(N_HEADS, PG), per_seq(NQ, D), per_seq(NQ, D),
                       pl.BlockSpec(tri_u.shape, lambda b, hp, j, pt: (0, 0))])
    assert len(prompt_specs) == N_PROMPT_IN
    grid_spec = pltpu.PrefetchScalarGridSpec(
        num_scalar_prefetch=1,
        grid=(B, HP, NJ),
        in_specs=prompt_specs + sample_specs,
        out_specs=[pl.BlockSpec((2, 1, LANES, T), lambda b, hp, j, pt: (0, b, hp, j)), per_seq(NQ, D)],
        scratch_shapes=[pltpu.VMEM((n, T, 2 * LANES), BF16), pltpu.VMEM((n, LANES, T), BF16),
                        pltpu.VMEM((LANES, D), BF16),
                        pltpu.VMEM((N_HEADS, LANES), F32),
                        pltpu.VMEM((LANES, LANES), F32), pltpu.VMEM((LANES, LANES), F32),
                        pltpu.VMEM((D, LANES), F32)],
    )
    return pl.pallas_call(
        _attn_kernel,
        grid_spec=grid_spec,
        out_shape=[jax.ShapeDtypeStruct((2, B, D, S // 2), BF16), jax.ShapeDtypeStruct((NB, NQ, D), BF16)],
        compiler_params=pltpu.CompilerParams(
            dimension_semantics=("arbitrary", "arbitrary", "arbitrary"), vmem_limit_bytes=VMEM_LIMIT),
        name="attn",
    )(page_table, qT, qT, qfT, qfT, kT, kfT, vT, gateT, gateT,
      *([ckT] * P), *([cvT] * P), *([clT] * P), knewT, vnewT, lnewT, qs, gates, tri_u)


def _outproj_fm_kernel(ogT_ref, h_ref, w_ref, y_ref):
    y_ref[0] = h_ref[0] + lax.dot_general(ogT_ref[0, 0], w_ref[...], (((0,), (0,)), ((), ())),
                                          preferred_element_type=F32)


def _outproj_fm(og2, h, w):
    _, B, D, S2 = og2.shape
    T = ATT_T
    n = 2 * S2 // T
    og_map = lambda b, t: (t // (n // 2), b, 0, jnp.where(t < n // 2, t, n - 1 - t))
    return pl.pallas_call(
        _outproj_fm_kernel,
        grid=(B, n),
        in_specs=[pl.BlockSpec((1, 1, D, T), og_map), pl.BlockSpec((1, T, D), lambda b, t: (b, t, 0)),
                  pl.BlockSpec(w.shape, lambda b, t: (0, 0), pipeline_mode=pl.Buffered(1))],
        out_specs=pl.BlockSpec((1, T, D), lambda b, t: (b, t, 0)),
        out_shape=jax.ShapeDtypeStruct((B, 2 * S2, D), F32),
        compiler_params=pltpu.CompilerParams(
            dimension_semantics=("arbitrary", "arbitrary"), vmem_limit_bytes=VMEM_LIMIT),
        name="outproj_prompt",
    )(og2, h, w)


def _outproj_kernel(og_ref, h_ref, w_ref, y_ref):
    y_ref[...] = h_ref[...] + _dot(og_ref[...], w_ref[...])


def _outproj(og, h, w):
    R, D = h.shape
    T = min(OUT_T, R)
    row = pl.BlockSpec((T, D), lambda i: (i, 0))
    return pl.pallas_call(
        _outproj_kernel,
        grid=(R // T,),
        in_specs=[row, row, pl.BlockSpec(w.shape, lambda i: (0, 0), pipeline_mode=pl.Buffered(1))],
        out_specs=row,
        out_shape=jax.ShapeDtypeStruct((R, D), F32),
        compiler_params=pltpu.CompilerParams(
            dimension_semantics=("arbitrary",), vmem_limit_bytes=VMEM_LIMIT),
        name="outproj",
    )(og, h, w)


def _upper_tri(n):
    return (jnp.arange(n)[:, None] <= jnp.arange(n)[None, :]).astype(BF16)


def kernel(x_prompt, x_sample, state_conv, cache_k, cache_v, cache_logf, page_table,
           a_norm, a_w_in, a_conv, a_w_out, kv_norm, kv_w, kv_fbias, k_norm,
           b_norm, b_w_in, q_norm, b_w_out):
    B, S, D = x_prompt.shape
    NB, NQ, _ = x_sample.shape
    n_pool, PG = cache_k.shape[0], cache_k.shape[1]
    H, hd = N_HEADS, HEAD_DIM
    assert a_norm.shape[0] == 1 and b_norm.shape[0] == 1, "one conv layer then one attention layer"
    assert D == D_MODEL and NQ == SUBLANES and PG == LANES

    common = [a_norm[0][None], a_w_in[0].astype(BF16), a_conv[0], a_w_out[0].astype(BF16),
              kv_norm[None], b_norm[0][None]]
    b_w_in_b = b_w_in[0].astype(BF16)
    kv_wT = kv_w.T
    kv_wT_ext = jnp.concatenate([kv_wT[:2 * D], jnp.repeat(kv_wT[2 * D:], AUG, axis=0), kv_wT[2 * D:]], axis=0)
    fbias_col = jnp.broadcast_to(jnp.concatenate([jnp.repeat(kv_fbias, AUG), kv_fbias])[:, None],
                                 (LANES + H, LANES))
    k_norm_col = jnp.broadcast_to(jnp.tile(k_norm, H)[:, None], (D, LANES))
    q_norm_col = jnp.broadcast_to(jnp.tile(q_norm[0], H)[:, None], (D, LANES))
    w_prompt = common + [kv_wT_ext.astype(BF16), fbias_col, k_norm_col, _upper_tri(PRE_T), b_w_in_b.T, q_norm_col]
    kv_w_row = jnp.concatenate([kv_w, jnp.zeros((D, LANES - H), kv_w.dtype)], axis=1)
    fbias_row = jnp.concatenate([kv_fbias, jnp.zeros((LANES - H,), kv_fbias.dtype)])[None]
    w_sample = common + [kv_w_row.astype(BF16), fbias_row, jnp.tile(k_norm, 2)[None], b_w_in_b,
                         jnp.tile(q_norm[0], 2)[None]]
    w_out_b = b_w_out[0].astype(BF16)

    (h1, kT, vT, lfT, qT, qfT, kfT, gateT, tail) = _pre_prompt(x_prompt, w_prompt)
    st = state_conv[0]
    p1 = jnp.pad(st[:, 1:2], ((0, 0), (0, NQ - 1), (0, 0))).reshape(NB * NQ, D)
    p2 = jnp.pad(st, ((0, 0), (0, NQ - 2), (0, 0))).reshape(NB * NQ, D)
    (h1s, k_s, v_s, lf_s, qs, gate_s, cv_s) = _pre_sample(x_sample.reshape(NB * NQ, D), p1, p2, w_sample)

    new_page = lambda a: jnp.pad(a.reshape(NB, NQ, -1), ((0, 0), (0, PG - NQ), (0, 0))).transpose(0, 2, 1)
    og2, ogs = _attention(
        qT, qfT, kT, kfT, vT, gateT, page_table,
        cache_k.transpose(0, 2, 3, 1).reshape(n_pool, D, PG),
        cache_v.transpose(0, 2, 3, 1).reshape(n_pool, D, PG),
        cache_logf.transpose(0, 2, 1),
        new_page(k_s), new_page(v_s), new_page(lf_s),
        qs.reshape(NB, NQ, D), gate_s.reshape(NB, NQ, D), _upper_tri(PG))
    y_prompt = _outproj_fm(og2, h1, w_out_b)
    y_sample = _outproj(ogs.reshape(NB * NQ, D), h1s, w_out_b).reshape(NB, NQ, D)

    conv_prompt = tail[:, SUBLANES - (CONV_W - 1):][None]
    conv_sample = cv_s.reshape(NB, NQ, D)[:, NQ - (CONV_W - 1):][None]
    to_bshd = lambda t: t.reshape(B, H, hd, S).transpose(0, 3, 1, 2)
    return (y_prompt, y_sample, conv_prompt, conv_sample,
            to_bshd(kT), to_bshd(vT), lfT.transpose(0, 2, 1),
            k_s.reshape(NB, NQ, H, hd), v_s.reshape(NB, NQ, H, hd), lf_s.reshape(NB, NQ, H))
```

```python
import jax
import jax.numpy as jnp
from jax import lax
from jax.experimental import pallas as pl
from jax.experimental.pallas import tpu as pltpu

F32 = jnp.float32
BF16 = jnp.bfloat16

D_MODEL = 1024
N_HEADS = 16
HEAD_DIM = 64
CONV_W = 3
EPS = 1e-6
LOG2E = 1.4426950408889634
NEG_BIG = -1e30

LANES = 128
SUBLANES = 8
AUG = LANES // N_HEADS
VMEM_LIMIT = 56 * 1024 * 1024

PRE_T = 256
ATT_T = 256
ATT_WAVE = 3
OUT_T = 512
PAGES_PER_STEP = 8
SAMPLE_SUB = 4
SEQS_PER_BATCH_STEP = 4


def _dot(a, b):
    return jnp.dot(a, b, preferred_element_type=F32)


def _dot_nt(a, b):
    return lax.dot_general(a, b, (((1,), (1,)), ((), ())), preferred_element_type=F32)


def _tile_lanes(x, n):
    return jnp.concatenate([x] * n, axis=1) if n > 1 else x


def _split3(x):
    hi = x.astype(BF16)
    r1 = x - hi.astype(F32)
    mid = r1.astype(BF16)
    r2 = r1 - mid.astype(F32)
    return hi, mid, r2.astype(BF16)


def _dot_exact_rhs(m_bf16, x_f32):
    hi, mid, lo = _split3(x_f32)
    return _dot(m_bf16, hi) + _dot(m_bf16, mid) + _dot(m_bf16, lo)


def _dot_exact_lhs(x_f32, m_bf16):
    hi, mid, lo = _split3(x_f32)
    return _dot(hi, m_bf16) + _dot(mid, m_bf16) + _dot(lo, m_bf16)


def _rms(x, g):
    r = lax.rsqrt(jnp.mean(x * x, axis=-1, keepdims=True) + EPS)
    return x * r * g


def _head_rms(x, g128):
    lane = lax.broadcasted_iota(jnp.int32, (1, LANES), 1)
    lo_half = lane < HEAD_DIM
    outs = []
    for p in range(x.shape[1] // LANES):
        xp = x[:, p * LANES:(p + 1) * LANES]
        sq = xp * xp
        s_lo = jnp.sum(jnp.where(lo_half, sq, 0.0), axis=-1, keepdims=True)
        s_hi = jnp.sum(jnp.where(lo_half, 0.0, sq), axis=-1, keepdims=True)
        r = jnp.where(lo_half, lax.rsqrt(s_lo / HEAD_DIM + EPS), lax.rsqrt(s_hi / HEAD_DIM + EPS))
        outs.append(xp * r * g128)
    return jnp.concatenate(outs, axis=1)


def _head_rms_fm(xT, gT):
    n, T = xT.shape
    x3 = xT.reshape(n // HEAD_DIM, HEAD_DIM, T)
    r = lax.rsqrt(jnp.mean(x3 * x3, axis=1, keepdims=True) + EPS)
    return (x3 * r).reshape(n, T) * gT


def _silu(z):
    return z * jax.nn.sigmoid(z)


def _log_sigmoid(x):
    return jnp.minimum(x, 0.0) - jnp.log1p(jnp.exp(-jnp.abs(x)))


def _bias_select(F, j, key_side):
    hi, mid, lo = (t.astype(F32) for t in _split3(F))
    if key_side:
        out = jnp.where(j < 3, 1.0, jnp.where(j == 3, -hi, jnp.where(j == 4, -mid, jnp.where(j == 5, -lo, 0.0))))
    else:
        out = jnp.where(j == 0, hi, jnp.where(j == 1, mid, jnp.where(j == 2, lo, jnp.where(j < 6, 1.0, 0.0))))
    return out.astype(BF16)


def _conv_layer(x, prev1, prev2, m1, m2, w):
    D = D_MODEL
    xn = _rms(x, w['a_norm'][...]).astype(BF16)
    w_in = w['a_w_in']
    v = _dot(xn, w_in[:, D:2 * D]) * _dot(xn, w_in[:, 2 * D:3 * D])
    v1 = jnp.where(m1, prev1, pltpu.roll(v, 1, axis=0))
    v2 = jnp.where(m2, prev2, pltpu.roll(v, 2, axis=0))
    cw = w['a_conv'][...]
    conv = v2 * cw[0:1] + v1 * cw[1:2] + v * cw[2:3]
    b = _dot(xn, w_in[:, 0:D])
    z = _dot(xn, w_in[:, 3 * D:4 * D])
    g = (_silu(z) * b * conv).astype(BF16)
    return v, x + _dot(g, w['a_w_out'][...])


def _fox_proj(h1, w):
    D = D_MODEL
    xn = _rms(h1, w['b_norm'][...]).astype(BF16)
    b_w_in = w['b_w_in']
    q = _head_rms(_dot(xn, b_w_in[:, 0:D]), w['q_norm'][...]) * (LOG2E * HEAD_DIM ** -0.5)
    gate = _silu(_dot(xn, b_w_in[:, D:2 * D]))
    return q.astype(BF16), gate.astype(BF16)


_W_COMMON = ('a_norm', 'a_w_in', 'a_conv', 'a_w_out', 'kv_norm', 'b_norm')
_W_PROMPT = _W_COMMON + ('kv_wT', 'fbias_col', 'k_norm_col', 'tri_u', 'b_w_inT', 'q_norm_col')
_W_SAMPLE = _W_COMMON + ('kv_w', 'fbias_row', 'k_norm_row', 'b_w_in', 'q_norm')


def _pre_prompt_kernel(x_ref, *refs):
    nw = len(_W_PROMPT)
    w = dict(zip(_W_PROMPT, refs[:nw]))
    (h1_ref, kT_ref, vT_ref, lfT_ref, qT_ref, qfT_ref, kfT_ref, gateT_ref, tail_ref, vcarry, fcarry) = refs[nw:]
    T = x_ref.shape[1]
    D = D_MODEL
    reps = T // LANES

    @pl.when(pl.program_id(1) == 0)
    def _():
        vcarry[...] = jnp.zeros_like(vcarry)
        fcarry[...] = jnp.zeros_like(fcarry)

    row = lax.broadcasted_iota(jnp.int32, (T, 1), 0)
    prev = vcarry[...]
    pm1 = prev[SUBLANES - 1:SUBLANES]
    pm2 = prev[SUBLANES - 2:SUBLANES - 1]
    v, h1 = _conv_layer(x_ref[0], pm1, jnp.where(row == 0, pm2, pm1), row == 0, row < 2, w)
    tail = v[T - SUBLANES:T]
    vcarry[...] = tail
    tail_ref[0] = tail
    h1_ref[0] = h1

    hn = _rms(h1, w['kv_norm'][...]).astype(BF16)
    kvT = w['kv_wT']
    kT_ref[0] = _head_rms_fm(_dot_nt(kvT[0:D, :], hn), _tile_lanes(w['k_norm_col'][...], reps))
    vT_ref[0] = _dot_nt(kvT[D:2 * D, :], hn)
    fT = _dot_nt(kvT[2 * D:2 * D + LANES + N_HEADS, :], hn) + _tile_lanes(w['fbias_col'][...], reps)
    lfT = _log_sigmoid(fT)
    lfT_ref[0] = lfT[LANES:LANES + N_HEADS]
    FT = _dot_exact_lhs(lfT[0:LANES], w['tri_u'][...]) + _tile_lanes(fcarry[...], reps)
    fcarry[...] = jnp.broadcast_to(FT[:, T - 1:T], fcarry.shape)
    Fp = FT * LOG2E
    slot = lax.broadcasted_iota(jnp.int32, (LANES, 1), 0) % AUG
    kfT_ref[0] = _bias_select(Fp, slot, True)
    qfT_ref[0] = _bias_select(Fp, slot, False)

    xn2 = _rms(h1, w['b_norm'][...]).astype(BF16)
    b_w_inT = w['b_w_inT']
    qT = _head_rms_fm(_dot_nt(b_w_inT[0:D, :], xn2), _tile_lanes(w['q_norm_col'][...], reps))
    qT_ref[0] = (qT * (LOG2E * HEAD_DIM ** -0.5)).astype(BF16)
    gateT_ref[0] = _silu(_dot_nt(b_w_inT[D:2 * D, :], xn2)).astype(BF16)


def _pre_sample_kernel(x_ref, p1_ref, p2_ref, *refs):
    nw = len(_W_SAMPLE)
    w = dict(zip(_W_SAMPLE, refs[:nw]))
    h1_ref, k_ref, v_ref, lf_ref, q_ref, gate_ref, cv_ref = refs[nw:]
    D = D_MODEL
    t = lax.broadcasted_iota(jnp.int32, (x_ref.shape[0], 1), 0) % SUBLANES
    v, h1 = _conv_layer(x_ref[...], p1_ref[...], p2_ref[...], t == 0, t < 2, w)
    cv_ref[...] = v
    h1_ref[...] = h1
    hn = _rms(h1, w['kv_norm'][...]).astype(BF16)
    kv_w = w['kv_w']
    k_ref[...] = _head_rms(_dot(hn, kv_w[:, 0:D]), w['k_norm_row'][...])
    v_ref[...] = _dot(hn, kv_w[:, D:2 * D])
    lfe = _log_sigmoid(_dot(hn, kv_w[:, 2 * D:2 * D + LANES]) + w['fbias_row'][...])
    lf_ref[...] = lfe[:, 0:N_HEADS]
    q, gate = _fox_proj(h1, w)
    q_ref[...] = q
    gate_ref[...] = gate


def _pre_prompt(x, weights):
    B, S, D = x.shape
    T = PRE_T
    tok = lambda d: pl.BlockSpec((1, T, d), lambda b, j: (b, j, 0))
    feat = lambda n: pl.BlockSpec((1, n, T), lambda b, j: (b, 0, j))
    out_shape = [
        jax.ShapeDtypeStruct((B, S, D), F32),
        jax.ShapeDtypeStruct((B, D, S), F32),
        jax.ShapeDtypeStruct((B, D, S), F32),
        jax.ShapeDtypeStruct((B, N_HEADS, S), F32),
        jax.ShapeDtypeStruct((B, D, S), BF16),
        jax.ShapeDtypeStruct((B, LANES, S), BF16),
        jax.ShapeDtypeStruct((B, LANES, S), BF16),
        jax.ShapeDtypeStruct((B, D, S), BF16),
        jax.ShapeDtypeStruct((B, SUBLANES, D), F32),
    ]
    out_specs = [tok(D), feat(D), feat(D), feat(N_HEADS), feat(D), feat(LANES), feat(LANES), feat(D),
                 pl.BlockSpec((1, SUBLANES, D), lambda b, j: (b, 0, 0))]
    const = lambda b, j: (0, 0)
    w_specs = [pl.BlockSpec(w.shape, const, pipeline_mode=pl.Buffered(1)) for w in weights]
    return pl.pallas_call(
        _pre_prompt_kernel,
        grid=(B, S // T),
        in_specs=[tok(D)] + w_specs,
        out_specs=out_specs,
        out_shape=out_shape,
        scratch_shapes=[pltpu.VMEM((SUBLANES, D), F32), pltpu.VMEM((LANES, LANES), F32)],
        compiler_params=pltpu.CompilerParams(
            dimension_semantics=("arbitrary", "arbitrary"), vmem_limit_bytes=VMEM_LIMIT),
        name="pre_prompt",
    )(x, *weights)


def _pre_sample(x2, p1, p2, weights):
    R, D = x2.shape
    out_shape = [
        jax.ShapeDtypeStruct((R, D), F32),
        jax.ShapeDtypeStruct((R, D), F32),
        jax.ShapeDtypeStruct((R, D), F32),
        jax.ShapeDtypeStruct((R, N_HEADS), F32),
        jax.ShapeDtypeStruct((R, D), BF16),
        jax.ShapeDtypeStruct((R, D), BF16),
        jax.ShapeDtypeStruct((R, D), F32),
    ]
    return pl.pallas_call(
        _pre_sample_kernel,
        out_shape=out_shape,
        compiler_params=pltpu.CompilerParams(vmem_limit_bytes=VMEM_LIMIT),
        name="pre_sample",
    )(x2, p1, p2, *weights)


def _prompt_attn_step(hp, j, qa_ref, qb_ref, qfa_ref, qfb_ref, kT_ref, kfT_ref, vT_ref, ga_ref, gb_ref,
                      o_ref, kbuf, vbuf, alongside):
    T = qa_ref.shape[2]
    n = kbuf.shape[0]

    @pl.when(j == 0)
    def _():
        for kt in range(n):
            cols = slice(kt * T, (kt + 1) * T)
            kbuf[kt, :, 0:LANES] = kT_ref[0, :, cols].T.astype(BF16)
            kbuf[kt, :, LANES:2 * LANES] = kfT_ref[0, :, cols].astype(F32).T.astype(BF16)
            vbuf[kt] = vT_ref[0, :, cols].astype(BF16)

    row = lax.broadcasted_iota(jnp.int32, (LANES, 1), 0)
    lo_half = row < HEAD_DIM

    def query_operand(q_ref, qf_ref):
        q2 = q_ref[0]
        qf = qf_ref[0]
        zero = jnp.zeros_like(q2)
        return jnp.concatenate([
            jnp.concatenate([jnp.where(lo_half, q2, zero), jnp.where(row // AUG == 2 * hp, qf, zero)], axis=0),
            jnp.concatenate([jnp.where(lo_half, zero, q2), jnp.where(row // AUG == 2 * hp + 1, qf, zero)], axis=0),
        ], axis=1)

    def scores(rhs, tiles, n_tiles):
        ss = [_dot(kbuf[kt], rhs) for kt in tiles]
        if tiles[-1] == n_tiles - 1:
            r = lax.broadcasted_iota(jnp.int32, ss[-1].shape, 0)
            c = lax.broadcasted_iota(jnp.int32, ss[-1].shape, 1)
            ss[-1] = jnp.where(r <= jnp.where(c >= T, c - T, c), ss[-1], NEG_BIG)
        m = None
        for s in ss:
            ms = jnp.max(s, axis=0, keepdims=True)
            m = ms if m is None else jnp.maximum(m, ms)
        return ss, m

    def run(problems):
        waves = []
        for pi, pr in enumerate(problems):
            for c0 in range(0, pr[3], ATT_WAVE):
                waves.append((pi, list(range(c0, min(c0 + ATT_WAVE, pr[3])))))
        rhs = [query_operand(pr[0], pr[1]) for pr in problems]
        state = [None] * len(problems)
        cur = scores(rhs[waves[0][0]], waves[0][1], problems[waves[0][0]][3])
        for w, (pi, tiles) in enumerate(waves):
            nxt = None
            if w + 1 < len(waves):
                npi, ntiles = waves[w + 1]
                nxt = scores(rhs[npi], ntiles, problems[npi][3])
            ss, m_w = cur
            m_new = m_w if state[pi] is None else jnp.maximum(state[pi][0], m_w)
            acc_w = l_w = None
            for kt, s in zip(tiles, ss):
                p = jnp.exp2(s - m_new)
                ps = jnp.sum(p, axis=0, keepdims=True)
                l_w = ps if l_w is None else l_w + ps
                v = vbuf[kt]
                if nxt is not None:
                    keep = (pltpu.bitcast(nxt[1][:, 0:T], jnp.int32) | 1) != 0
                    v = jnp.where(keep, v, jnp.zeros_like(v))
                d = _dot(v, p.astype(BF16))
                acc_w = d if acc_w is None else acc_w + d
            if state[pi] is None:
                state[pi] = (m_new, l_w, acc_w)
            else:
                alpha = jnp.exp2(state[pi][0] - m_new)
                state[pi] = (m_new, alpha * state[pi][1] + l_w, alpha * state[pi][2] + acc_w)
            cur = nxt
        for pr, (_, l, acc) in zip(problems, state):
            o = acc / l
            o = jnp.where(lo_half, o[:, 0:T], o[:, T:2 * T])
            o_ref[pr[4], 0] = (o * pr[2][0].astype(F32)).astype(BF16)

    for v in range(n // 2):
        @pl.when(j == v)
        def _(v=v):
            alongside()
            run([(qa_ref, qfa_ref, ga_ref, v + 1, 0), (qb_ref, qfb_ref, gb_ref, n - v, 1)])


def _sample_attn_parts(c, n_chunks, refs):
    P = PAGES_PER_STEP
    kpages = refs[0:P]
    vpages = refs[P:2 * P]
    lpages = refs[2 * P:3 * P]
    (knew_ref, vnew_ref, lnew_ref, q_ref, gate_ref, tri_ref,
     o_ref, qblk, gcarry, m_scr, l_scr, acc_scr) = refs[3 * P:]
    nq = q_ref.shape[1]
    R = LANES
    PG = LANES
    D = D_MODEL

    def begin():
        q = q_ref[0]
        rows = lax.broadcasted_iota(jnp.int32, (R, D), 0)
        cols = lax.broadcasted_iota(jnp.int32, (R, D), 1)
        tiled = jnp.concatenate([q] * (R // nq), axis=0)
        qblk[...] = jnp.where(cols // HEAD_DIM == rows // nq, tiled, jnp.zeros_like(tiled))
        gcarry[...] = jnp.zeros_like(gcarry)
        m_scr[...] = jnp.full_like(m_scr, NEG_BIG)
        l_scr[...] = jnp.zeros_like(l_scr)
        acc_scr[...] = jnp.zeros_like(acc_scr)

    def bias_rows(lf_pages):
        cs = _dot_exact_lhs(jnp.concatenate(lf_pages, axis=0), tri_ref[...])
        carry = gcarry[...]
        out = []
        for j in range(len(lf_pages)):
            G = cs[j * N_HEADS:(j + 1) * N_HEADS] + carry
            carry = jnp.broadcast_to(G[:, PG - 1:PG], carry.shape)
            out.append(jnp.concatenate(
                [jnp.broadcast_to(G[h:h + 1], (nq, PG)) for h in range(N_HEADS)], axis=0))
        gcarry[...] = carry
        return jnp.concatenate(out, axis=1) if len(out) > 1 else out[0]

    def update(s, vT):
        m_prev = m_scr[...]
        m_new = jnp.maximum(m_prev, jnp.max(s, axis=1, keepdims=True))
        alpha = jnp.exp2(m_prev - m_new)
        p = jnp.exp2(s - _tile_lanes(m_new, s.shape[1] // LANES))
        l_scr[...] = alpha * l_scr[...] + jnp.sum(p, axis=1, keepdims=True)
        acc_scr[...] = jnp.concatenate([alpha.T] * (D // LANES), axis=0) * acc_scr[...] + _dot_nt(vT, p.astype(BF16))
        m_scr[...] = m_new

    def chunk():
        bias = bias_rows([lpages[j][0] for j in range(P)]) * LOG2E
        subs = []
        for h0 in range(0, P, SAMPLE_SUB):
            kT = jnp.concatenate([kpages[j][0].astype(BF16) for j in range(h0, h0 + SAMPLE_SUB)], axis=1)
            subs.append(_dot(qblk[...], kT) - bias[:, h0 * PG:(h0 + SAMPLE_SUB) * PG])
        for h0, s in zip(range(0, P, SAMPLE_SUB), subs):
            update(s, jnp.concatenate([vpages[j][0].astype(BF16) for j in range(h0, h0 + SAMPLE_SUB)], axis=1))

    def finish():
        s = _dot(qblk[...], knew_ref[0].astype(BF16)) - bias_rows([lnew_ref[0]]) * LOG2E
        key = lax.broadcasted_iota(jnp.int32, s.shape, 1)
        qi = lax.broadcasted_iota(jnp.int32, s.shape, 0) % nq
        update(jnp.where(key <= qi, s, NEG_BIG), vnew_ref[0].astype(BF16))

        linv = 1.0 / l_scr[...]
        oT = acc_scr[...] * jnp.concatenate([linv.T] * (D // LANES), axis=0)
        rows = lax.broadcasted_iota(jnp.int32, oT.shape, 0)
        cols = lax.broadcasted_iota(jnp.int32, oT.shape, 1)
        oT = jnp.where(rows // HEAD_DIM == cols // nq, oT, 0.0)
        shift = nq
        while shift < LANES:
            oT = oT + pltpu.roll(oT, shift, axis=1)
            shift *= 2
        o = oT.T[0:nq]
        o_ref[0] = (o * gate_ref[0].astype(F32)).astype(BF16)

    return begin, chunk, finish


N_PROMPT_IN = 9


def _attn_kernel(pt_ref, *refs):
    del pt_ref
    n_sample_in = 3 * PAGES_PER_STEP + 6
    prompt_in = refs[:N_PROMPT_IN]
    sample_in = refs[N_PROMPT_IN:N_PROMPT_IN + n_sample_in]
    o_ref, os_ref, kbuf, vbuf, qblk, gcarry, m_scr, l_scr, acc_scr = refs[N_PROMPT_IN + n_sample_in:]
    hp = pl.program_id(1)
    j = pl.program_id(2)
    nj = kbuf.shape[0] // 2
    n_chunks = (D_MODEL // LANES) * nj // SEQS_PER_BATCH_STEP
    c = (hp * nj + j) % n_chunks
    begin, chunk, finish = _sample_attn_parts(
        c, n_chunks, sample_in + (os_ref, qblk, gcarry, m_scr, l_scr, acc_scr))
    pl.when(c == 0)(begin)
    _prompt_attn_step(hp, j, *prompt_in, o_ref, kbuf, vbuf, chunk)
    pl.when(c == n_chunks - 1)(finish)


def _attention(qT, qfT, kT, kfT, vT, gateT, page_table, ckT, cvT, clT, knewT, vnewT, lnewT, qs, gates, tri_u):
    B, D, S = qT.shape
    NB, NQ, _ = qs.shape
    T = ATT_T
    n = S // T
    HP, NJ = D // LANES, n // 2
    P = PAGES_PER_STEP
    PG = ckT.shape[2]
    n_chunks = page_table.shape[1] // P
    assert HP * NJ == SEQS_PER_BATCH_STEP * n_chunks and NB == B * SEQS_PER_BATCH_STEP

    lo = lambda rows: pl.BlockSpec((1, LANES, T), lambda b, hp, j, pt: (b, hp if rows else 0, j))
    hi = lambda rows: pl.BlockSpec((1, LANES, T), lambda b, hp, j, pt: (b, hp if rows else 0, n - 1 - j))
    kspec = pl.BlockSpec((1, LANES, S), lambda b, hp, j, pt: (b, hp, 0))
    prompt_specs = [lo(True), hi(True), lo(False), hi(False),
                    kspec, pl.BlockSpec((1, LANES, S), lambda b, hp, j, pt: (b, 0, 0)), kspec, lo(True), hi(True)]

    seq = lambda b, hp, j: b * SEQS_PER_BATCH_STEP + (hp * NJ + j) // n_chunks

    def page_spec(rows, jj):
        return pl.BlockSpec(
            (1, rows, PG),
            lambda b, hp, j, pt: (pt[seq(b, hp, j), ((hp * NJ + j) % n_chunks) * P + jj], 0, 0))

    per_seq = lambda rows, width: pl.BlockSpec((1, rows, width), lambda b, hp, j, pt: (seq(b, hp, j), 0, 0))
    sample_specs = ([page_spec(D, jj) for jj in range(P)] + [page_spec(D, jj) for jj in range(P)]
                    + [page_spec(N_HEADS, jj) for jj in range(P)]
                    + [per_seq(D, PG), per_seq(D, PG), per_seq(N_HEADS, PG), per_seq(NQ, D), per_seq(NQ, D),
                       pl.BlockSpec(tri_u.shape, lambda b, hp, j, pt: (0, 0))])
    assert len(prompt_specs) == N_PROMPT_IN
    grid_spec = pltpu.PrefetchScalarGridSpec(
        num_scalar_prefetch=1,
        grid=(B, HP, NJ),
        in_specs=prompt_specs + sample_specs,
        out_specs=[pl.BlockSpec((2, 1, LANES, T), lambda b, hp, j, pt: (0, b, hp, j)), per_seq(NQ, D)],
        scratch_shapes=[pltpu.VMEM((n, T, 2 * LANES), BF16), pltpu.VMEM((n, LANES, T), BF16),
                        pltpu.VMEM((LANES, D), BF16),
                        pltpu.VMEM((N_HEADS, LANES), F32),
                        pltpu.VMEM((LANES, LANES), F32), pltpu.VMEM((LANES, LANES), F32),
                        pltpu.VMEM((D, LANES), F32)],
    )
    return pl.pallas_call(
        _attn_kernel,
        grid_spec=grid_spec,
        out_shape=[jax.ShapeDtypeStruct((2, B, D, S // 2), BF16), jax.ShapeDtypeStruct((NB, NQ, D), BF16)],
        compiler_params=pltpu.CompilerParams(
            dimension_semantics=("arbitrary", "arbitrary", "arbitrary"), vmem_limit_bytes=VMEM_LIMIT),
        name="attn",
    )(page_table, qT, qT, qfT, qfT, kT, kfT, vT, gateT, gateT,
      *([ckT] * P), *([cvT] * P), *([clT] * P), knewT, vnewT, lnewT, qs, gates, tri_u)


def _outproj_fm_kernel(ogT_ref, h_ref, w_ref, y_ref):
    T = ATT_T
    og = ogT_ref[0, 0]
    swapped = pl.program_id(1) >= pl.num_programs(1) // 2
    contract = lambda a: lax.dot_general(a, w_ref[...], (((0,), (0,)), ((), ())), preferred_element_type=F32)
    y_ref[0, 0:T] = h_ref[0, 0:T] + contract(jnp.where(swapped, og[:, T:2 * T], og[:, 0:T]))
    y_ref[0, T:2 * T] = h_ref[0, T:2 * T] + contract(jnp.where(swapped, og[:, 0:T], og[:, T:2 * T]))


def _outproj_fm(og2, h, w):
    _, B, D, S2 = og2.shape
    T = 2 * ATT_T
    n = 2 * S2 // T
    og_map = lambda b, t: (t // (n // 2), b, 0, jnp.where(t < n // 2, t, n - 1 - t))
    return pl.pallas_call(
        _outproj_fm_kernel,
        grid=(B, n),
        in_specs=[pl.BlockSpec((1, 1, D, T), og_map), pl.BlockSpec((1, T, D), lambda b, t: (b, t, 0)),
                  pl.BlockSpec(w.shape, lambda b, t: (0, 0), pipeline_mode=pl.Buffered(1))],
        out_specs=pl.BlockSpec((1, T, D), lambda b, t: (b, t, 0)),
        out_shape=jax.ShapeDtypeStruct((B, 2 * S2, D), F32),
        compiler_params=pltpu.CompilerParams(
            dimension_semantics=("arbitrary", "arbitrary"), vmem_limit_bytes=VMEM_LIMIT),
        name="outproj_prompt",
    )(og2, h, w)


def _outproj_kernel(og_ref, h_ref, w_ref, y_ref):
    y_ref[...] = h_ref[...] + _dot(og_ref[...], w_ref[...])


def _outproj(og, h, w):
    R, D = h.shape
    T = min(OUT_T, R)
    row = pl.BlockSpec((T, D), lambda i: (i, 0))
    return pl.pallas_call(
        _outproj_kernel,
        grid=(R // T,),
        in_specs=[row, row, pl.BlockSpec(w.shape, lambda i: (0, 0), pipeline_mode=pl.Buffered(1))],
        out_specs=row,
        out_shape=jax.ShapeDtypeStruct((R, D), F32),
        compiler_params=pltpu.CompilerParams(
            dimension_semantics=("arbitrary",), vmem_limit_bytes=VMEM_LIMIT),
        name="outproj",
    )(og, h, w)


def _upper_tri(n):
    return (jnp.arange(n)[:, None] <= jnp.arange(n)[None, :]).astype(BF16)


def kernel(x_prompt, x_sample, state_conv, cache_k, cache_v, cache_logf, page_table,
           a_norm, a_w_in, a_conv, a_w_out, kv_norm, kv_w, kv_fbias, k_norm,
           b_norm, b_w_in, q_norm, b_w_out):
    B, S, D = x_prompt.shape
    NB, NQ, _ = x_sample.shape
    n_pool, PG = cache_k.shape[0], cache_k.shape[1]
    H, hd = N_HEADS, HEAD_DIM
    assert a_norm.shape[0] == 1 and b_norm.shape[0] == 1, "one conv layer then one attention layer"
    assert D == D_MODEL and NQ == SUBLANES and PG == LANES

    common = [a_norm[0][None], a_w_in[0].astype(BF16), a_conv[0], a_w_out[0].astype(BF16),
              kv_norm[None], b_norm[0][None]]
    b_w_in_b = b_w_in[0].astype(BF16)
    kv_wT = kv_w.T
    kv_wT_ext = jnp.concatenate([kv_wT[:2 * D], jnp.repeat(kv_wT[2 * D:], AUG, axis=0), kv_wT[2 * D:]], axis=0)
    fbias_col = jnp.broadcast_to(jnp.concatenate([jnp.repeat(kv_fbias, AUG), kv_fbias])[:, None],
                                 (LANES + H, LANES))
    k_norm_col = jnp.broadcast_to(jnp.tile(k_norm, H)[:, None], (D, LANES))
    q_norm_col = jnp.broadcast_to(jnp.tile(q_norm[0], H)[:, None], (D, LANES))
    w_prompt = common + [kv_wT_ext.astype(BF16), fbias_col, k_norm_col, _upper_tri(PRE_T), b_w_in_b.T, q_norm_col]
    kv_w_row = jnp.concatenate([kv_w, jnp.zeros((D, LANES - H), kv_w.dtype)], axis=1)
    fbias_row = jnp.concatenate([kv_fbias, jnp.zeros((LANES - H,), kv_fbias.dtype)])[None]
    w_sample = common + [kv_w_row.astype(BF16), fbias_row, jnp.tile(k_norm, 2)[None], b_w_in_b,
                         jnp.tile(q_norm[0], 2)[None]]
    w_out_b = b_w_out[0].astype(BF16)

    (h1, kT, vT, lfT, qT, qfT, kfT, gateT, tail) = _pre_prompt(x_prompt, w_prompt)
    st = state_conv[0]
    p1 = jnp.pad(st[:, 1:2], ((0, 0), (0, NQ - 1), (0, 0))).reshape(NB * NQ, D)
    p2 = jnp.pad(st, ((0, 0), (0, NQ - 2), (0, 0))).reshape(NB * NQ, D)
    (h1s, k_s, v_s, lf_s, qs, gate_s, cv_s) = _pre_sample(x_sample.reshape(NB * NQ, D), p1, p2, w_sample)

    new_page = lambda a: jnp.pad(a.reshape(NB, NQ, -1), ((0, 0), (0, PG - NQ), (0, 0))).transpose(0, 2, 1)
    og2, ogs = _attention(
        qT, qfT, kT, kfT, vT, gateT, page_table,
        cache_k.transpose(0, 2, 3, 1).reshape(n_pool, D, PG),
        cache_v.transpose(0, 2, 3, 1).reshape(n_pool, D, PG),
        cache_logf.transpose(0, 2, 1),
        new_page(k_s), new_page(v_s), new_page(lf_s),
        qs.reshape(NB, NQ, D), gate_s.reshape(NB, NQ, D), _upper_tri(PG))
    y_prompt = _outproj_fm(og2, h1, w_out_b)
    y_sample = _outproj(ogs.reshape(NB * NQ, D), h1s, w_out_b).reshape(NB, NQ, D)

    conv_prompt = tail[:, SUBLANES - (CONV_W - 1):][None]
    conv_sample = cv_s.reshape(NB, NQ, D)[:, NQ - (CONV_W - 1):][None]
    to_bshd = lambda t: t.reshape(B, H, hd, S).transpose(0, 3, 1, 2)
    return (y_prompt, y_sample, conv_prompt, conv_sample,
            to_bshd(kT), to_bshd(vT), lfT.transpose(0, 2, 1),
            k_s.reshape(NB, NQ, H, hd), v_s.reshape(NB, NQ, H, hd), lf_s.reshape(NB, NQ, H))
```

```python
import jax
import jax.numpy as jnp
from jax import lax
from jax.experimental import pallas as pl
from jax.experimental.pallas import tpu as pltpu

F32 = jnp.float32
BF16 = jnp.bfloat16

D_MODEL = 1024
N_HEADS = 16
HEAD_DIM = 64
CONV_W = 3
EPS = 1e-6
LOG2E = 1.4426950408889634
NEG_BIG = -1e30

LANES = 128
SUBLANES = 8
AUG = LANES // N_HEADS
VMEM_LIMIT = 56 * 1024 * 1024

PRE_T = 256
ATT_T = 256
ATT_WAVE = 3
OUT_T = 512
PRE_PAGES_PER_STEP = 8
PRE_CHUNKS = 2
PRE_PAGES = PRE_PAGES_PER_STEP * PRE_CHUNKS
ATT_PAGES_PER_STEP = 6
SEQS_PER_BATCH_STEP = 4


def _dot(a, b):
    return jnp.dot(a, b, preferred_element_type=F32)


def _dot_nt(a, b):
    return lax.dot_general(a, b, (((1,), (1,)), ((), ())), preferred_element_type=F32)


def _tile_lanes(x, n):
    return jnp.concatenate([x] * n, axis=1) if n > 1 else x


def _split3(x):
    hi = x.astype(BF16)
    r1 = x - hi.astype(F32)
    mid = r1.astype(BF16)
    r2 = r1 - mid.astype(F32)
    return hi, mid, r2.astype(BF16)


def _dot_exact_rhs(m_bf16, x_f32):
    hi, mid, lo = _split3(x_f32)
    return _dot(m_bf16, hi) + _dot(m_bf16, mid) + _dot(m_bf16, lo)


def _dot_exact_lhs(x_f32, m_bf16):
    hi, mid, lo = _split3(x_f32)
    return _dot(hi, m_bf16) + _dot(mid, m_bf16) + _dot(lo, m_bf16)


def _rms(x, g):
    r = lax.rsqrt(jnp.mean(x * x, axis=-1, keepdims=True) + EPS)
    return x * r * g


def _head_rms(x, g128):
    lane = lax.broadcasted_iota(jnp.int32, (1, LANES), 1)
    lo_half = lane < HEAD_DIM
    outs = []
    for p in range(x.shape[1] // LANES):
        xp = x[:, p * LANES:(p + 1) * LANES]
        sq = xp * xp
        s_lo = jnp.sum(jnp.where(lo_half, sq, 0.0), axis=-1, keepdims=True)
        s_hi = jnp.sum(jnp.where(lo_half, 0.0, sq), axis=-1, keepdims=True)
        r = jnp.where(lo_half, lax.rsqrt(s_lo / HEAD_DIM + EPS), lax.rsqrt(s_hi / HEAD_DIM + EPS))
        outs.append(xp * r * g128)
    return jnp.concatenate(outs, axis=1)


def _head_rms_fm(xT, gT):
    n, T = xT.shape
    x3 = xT.reshape(n // HEAD_DIM, HEAD_DIM, T)
    r = lax.rsqrt(jnp.mean(x3 * x3, axis=1, keepdims=True) + EPS)
    return (x3 * r).reshape(n, T) * gT


def _silu(z):
    return z * jax.nn.sigmoid(z)


def _log_sigmoid(x):
    return jnp.minimum(x, 0.0) - jnp.log1p(jnp.exp(-jnp.abs(x)))


def _bias_select(F, j, key_side):
    hi, mid, lo = (t.astype(F32) for t in _split3(F))
    if key_side:
        out = jnp.where(j < 3, 1.0, jnp.where(j == 3, -hi, jnp.where(j == 4, -mid, jnp.where(j == 5, -lo, 0.0))))
    else:
        out = jnp.where(j == 0, hi, jnp.where(j == 1, mid, jnp.where(j == 2, lo, jnp.where(j < 6, 1.0, 0.0))))
    return out.astype(BF16)


def _conv_layer(x, prev1, prev2, m1, m2, w):
    D = D_MODEL
    xn = _rms(x, w['a_norm'][...]).astype(BF16)
    w_in = w['a_w_in']
    v = _dot(xn, w_in[:, D:2 * D]) * _dot(xn, w_in[:, 2 * D:3 * D])
    v1 = jnp.where(m1, prev1, pltpu.roll(v, 1, axis=0))
    v2 = jnp.where(m2, prev2, pltpu.roll(v, 2, axis=0))
    cw = w['a_conv'][...]
    conv = v2 * cw[0:1] + v1 * cw[1:2] + v * cw[2:3]
    b = _dot(xn, w_in[:, 0:D])
    z = _dot(xn, w_in[:, 3 * D:4 * D])
    g = (_silu(z) * b * conv).astype(BF16)
    return v, x + _dot(g, w['a_w_out'][...])


def _fox_proj(h1, w):
    D = D_MODEL
    xn = _rms(h1, w['b_norm'][...]).astype(BF16)
    b_w_in = w['b_w_in']
    q = _head_rms(_dot(xn, b_w_in[:, 0:D]), w['q_norm'][...]) * (LOG2E * HEAD_DIM ** -0.5)
    gate = _silu(_dot(xn, b_w_in[:, D:2 * D]))
    return q.astype(BF16), gate.astype(BF16)


_W_COMMON = ('a_norm', 'a_w_in', 'a_conv', 'a_w_out', 'kv_norm', 'b_norm')
_W_PROMPT = _W_COMMON + ('kv_wT', 'fbias_col', 'k_norm_col', 'tri_u', 'b_w_inT', 'q_norm_col')
_W_SAMPLE = _W_COMMON + ('kv_w', 'fbias_row', 'k_norm_row', 'b_w_in', 'q_norm')


def _pre_prompt_kernel(pt_ref, x_ref, *refs):
    del pt_ref
    nw = len(_W_PROMPT)
    P = PRE_PAGES_PER_STEP
    w = dict(zip(_W_PROMPT, refs[:nw]))
    rest = refs[nw:]
    kpages, vpages, lpages = rest[0:P], rest[P:2 * P], rest[2 * P:3 * P]
    qs_ref, tri_ref = rest[3 * P:3 * P + 2]
    (h1_ref, kT_ref, vT_ref, lfT_ref, qT_ref, qfT_ref, kfT_ref, gateT_ref, tail_ref) = rest[3 * P + 2:3 * P + 11]
    state_out = rest[3 * P + 11:3 * P + 11 + N_STATE]
    vcarry, fcarry = rest[3 * P + 11 + N_STATE:3 * P + 13 + N_STATE]
    T = x_ref.shape[1]
    D = D_MODEL
    reps = T // LANES

    @pl.when(pl.program_id(1) == 0)
    def _():
        vcarry[...] = jnp.zeros_like(vcarry)
        fcarry[...] = jnp.zeros_like(fcarry)

    c = (pl.program_id(0) * pl.num_programs(1) + pl.program_id(1)) % PRE_CHUNKS
    begin, chunk, save, _ = _sample_attn_parts(kpages, vpages, lpages, qs_ref, tri_ref,
                                               rest[3 * P + 13 + N_STATE:])
    pl.when(c == 0)(lambda: begin(None))
    chunk()

    row = lax.broadcasted_iota(jnp.int32, (T, 1), 0)
    prev = vcarry[...]
    pm1 = prev[SUBLANES - 1:SUBLANES]
    pm2 = prev[SUBLANES - 2:SUBLANES - 1]
    v, h1 = _conv_layer(x_ref[0], pm1, jnp.where(row == 0, pm2, pm1), row == 0, row < 2, w)
    tail = v[T - SUBLANES:T]
    vcarry[...] = tail
    tail_ref[0] = tail
    h1_ref[0] = h1

    hn = _rms(h1, w['kv_norm'][...]).astype(BF16)
    kvT = w['kv_wT']
    kT_ref[0] = _head_rms_fm(_dot_nt(kvT[0:D, :], hn), _tile_lanes(w['k_norm_col'][...], reps))
    vT_ref[0] = _dot_nt(kvT[D:2 * D, :], hn)
    fT = _dot_nt(kvT[2 * D:2 * D + LANES + N_HEADS, :], hn) + _tile_lanes(w['fbias_col'][...], reps)
    lfT = _log_sigmoid(fT)
    lfT_ref[0] = lfT[LANES:LANES + N_HEADS]
    FT = _dot_exact_lhs(lfT[0:LANES], w['tri_u'][...]) + _tile_lanes(fcarry[...], reps)
    fcarry[...] = jnp.broadcast_to(FT[:, T - 1:T], fcarry.shape)
    Fp = FT * LOG2E
    slot = lax.broadcasted_iota(jnp.int32, (LANES, 1), 0) % AUG
    kfT_ref[0] = _bias_select(Fp, slot, True)
    qfT_ref[0] = _bias_select(Fp, slot, False)

    xn2 = _rms(h1, w['b_norm'][...]).astype(BF16)
    b_w_inT = w['b_w_inT']
    qT = _head_rms_fm(_dot_nt(b_w_inT[0:D, :], xn2), _tile_lanes(w['q_norm_col'][...], reps))
    qT_ref[0] = (qT * (LOG2E * HEAD_DIM ** -0.5)).astype(BF16)
    gateT_ref[0] = _silu(_dot_nt(b_w_inT[D:2 * D, :], xn2)).astype(BF16)

    pl.when(c == PRE_CHUNKS - 1)(lambda: save(state_out))


def _pre_sample_kernel(x_ref, p1_ref, p2_ref, *refs):
    nw = len(_W_SAMPLE)
    w = dict(zip(_W_SAMPLE, refs[:nw]))
    h1_ref, k_ref, v_ref, lf_ref, q_ref, gate_ref, cv_ref = refs[nw:]
    D = D_MODEL
    t = lax.broadcasted_iota(jnp.int32, (x_ref.shape[0], 1), 0) % SUBLANES
    v, h1 = _conv_layer(x_ref[...], p1_ref[...], p2_ref[...], t == 0, t < 2, w)
    cv_ref[...] = v
    h1_ref[...] = h1
    hn = _rms(h1, w['kv_norm'][...]).astype(BF16)
    kv_w = w['kv_w']
    k_ref[...] = _head_rms(_dot(hn, kv_w[:, 0:D]), w['k_norm_row'][...])
    v_ref[...] = _dot(hn, kv_w[:, D:2 * D])
    lfe = _log_sigmoid(_dot(hn, kv_w[:, 2 * D:2 * D + LANES]) + w['fbias_row'][...])
    lf_ref[...] = lfe[:, 0:N_HEADS]
    q, gate = _fox_proj(h1, w)
    q_ref[...] = q
    gate_ref[...] = gate


def _pre_prompt(x, weights, page_table, ckT, cvT, clT, qs, tri_u):
    B, S, D = x.shape
    NB, NQ, _ = qs.shape
    T = PRE_T
    NJ = S // T
    P = PRE_PAGES_PER_STEP
    PG = ckT.shape[2]
    assert B * NJ == NB * PRE_CHUNKS
    tok = lambda d: pl.BlockSpec((1, T, d), lambda b, j, pt: (b, j, 0))
    feat = lambda n: pl.BlockSpec((1, n, T), lambda b, j, pt: (b, 0, j))
    seq = lambda b, j: (b * NJ + j) // PRE_CHUNKS

    def page_spec(rows, jj):
        return pl.BlockSpec(
            (1, rows, PG), lambda b, j, pt: (pt[seq(b, j), ((b * NJ + j) % PRE_CHUNKS) * P + jj], 0, 0))

    per_seq = lambda rows, width: pl.BlockSpec((1, rows, width), lambda b, j, pt: (seq(b, j), 0, 0))
    sample_specs = ([page_spec(D, jj) for jj in range(P)] + [page_spec(D, jj) for jj in range(P)]
                    + [page_spec(N_HEADS, jj) for jj in range(P)]
                    + [per_seq(NQ, D), pl.BlockSpec(tri_u.shape, lambda b, j, pt: (0, 0))])
    state_shapes = _sample_state_shapes(NB)
    out_shape = [
        jax.ShapeDtypeStruct((B, S, D), F32),
        jax.ShapeDtypeStruct((B, D, S), F32),
        jax.ShapeDtypeStruct((B, D, S), F32),
        jax.ShapeDtypeStruct((B, N_HEADS, S), F32),
        jax.ShapeDtypeStruct((B, D, S), BF16),
        jax.ShapeDtypeStruct((B, LANES, S), BF16),
        jax.ShapeDtypeStruct((B, LANES, S), BF16),
        jax.ShapeDtypeStruct((B, D, S), BF16),
        jax.ShapeDtypeStruct((B, SUBLANES, D), F32),
    ]
    out_specs = [tok(D), feat(D), feat(D), feat(N_HEADS), feat(D), feat(LANES), feat(LANES), feat(D),
                 pl.BlockSpec((1, SUBLANES, D), lambda b, j, pt: (b, 0, 0))]
    out_specs += [per_seq(*st.shape[1:]) for st in state_shapes]
    const = lambda b, j, pt: (0, 0)
    w_specs = [pl.BlockSpec(w.shape, const, pipeline_mode=pl.Buffered(1)) for w in weights]
    grid_spec = pltpu.PrefetchScalarGridSpec(
        num_scalar_prefetch=1,
        grid=(B, NJ),
        in_specs=[tok(D)] + w_specs + sample_specs,
        out_specs=out_specs,
        scratch_shapes=[pltpu.VMEM((SUBLANES, D), F32), pltpu.VMEM((LANES, LANES), F32)] + _sample_scratch(),
    )
    outs = pl.pallas_call(
        _pre_prompt_kernel,
        grid_spec=grid_spec,
        out_shape=out_shape + state_shapes,
        compiler_params=pltpu.CompilerParams(
            dimension_semantics=("arbitrary", "arbitrary"), vmem_limit_bytes=VMEM_LIMIT),
        name="pre_prompt",
    )(page_table, x, *weights, *([ckT] * P), *([cvT] * P), *([clT] * P), qs, tri_u)
    return outs[:len(out_shape)], outs[len(out_shape):]


def _pre_sample(x2, p1, p2, weights):
    R, D = x2.shape
    out_shape = [
        jax.ShapeDtypeStruct((R, D), F32),
        jax.ShapeDtypeStruct((R, D), F32),
        jax.ShapeDtypeStruct((R, D), F32),
        jax.ShapeDtypeStruct((R, N_HEADS), F32),
        jax.ShapeDtypeStruct((R, D), BF16),
        jax.ShapeDtypeStruct((R, D), BF16),
        jax.ShapeDtypeStruct((R, D), F32),
    ]
    return pl.pallas_call(
        _pre_sample_kernel,
        out_shape=out_shape,
        compiler_params=pltpu.CompilerParams(vmem_limit_bytes=VMEM_LIMIT),
        name="pre_sample",
    )(x2, p1, p2, *weights)


def _prompt_attn_step(hp, j, qa_ref, qb_ref, qfa_ref, qfb_ref, kT_ref, kfT_ref, vT_ref, ga_ref, gb_ref,
                      o_ref, kbuf, vbuf, alongside):
    T = qa_ref.shape[2]
    n = kbuf.shape[0]

    @pl.when(j == 0)
    def _():
        for kt in range(n):
            cols = slice(kt * T, (kt + 1) * T)
            kbuf[kt, :, 0:LANES] = kT_ref[0, :, cols].T.astype(BF16)
            kbuf[kt, :, LANES:2 * LANES] = kfT_ref[0, :, cols].astype(F32).T.astype(BF16)
            vbuf[kt] = vT_ref[0, :, cols].astype(BF16)

    row = lax.broadcasted_iota(jnp.int32, (LANES, 1), 0)
    lo_half = row < HEAD_DIM

    def query_operand(q_ref, qf_ref):
        q2 = q_ref[0]
        qf = qf_ref[0]
        zero = jnp.zeros_like(q2)
        return jnp.concatenate([
            jnp.concatenate([jnp.where(lo_half, q2, zero), jnp.where(row // AUG == 2 * hp, qf, zero)], axis=0),
            jnp.concatenate([jnp.where(lo_half, zero, q2), jnp.where(row // AUG == 2 * hp + 1, qf, zero)], axis=0),
        ], axis=1)

    def scores(rhs, tiles, n_tiles):
        ss = [_dot(kbuf[kt], rhs) for kt in tiles]
        if tiles[-1] == n_tiles - 1:
            r = lax.broadcasted_iota(jnp.int32, ss[-1].shape, 0)
            c = lax.broadcasted_iota(jnp.int32, ss[-1].shape, 1)
            ss[-1] = jnp.where(r <= jnp.where(c >= T, c - T, c), ss[-1], NEG_BIG)
        m = None
        for s in ss:
            ms = jnp.max(s, axis=0, keepdims=True)
            m = ms if m is None else jnp.maximum(m, ms)
        return ss, m

    def run(problems):
        waves = []
        for pi, pr in enumerate(problems):
            for c0 in range(0, pr[3], ATT_WAVE):
                waves.append((pi, list(range(c0, min(c0 + ATT_WAVE, pr[3])))))
        rhs = [query_operand(pr[0], pr[1]) for pr in problems]
        state = [None] * len(problems)
        cur = scores(rhs[waves[0][0]], waves[0][1], problems[waves[0][0]][3])
        for w, (pi, tiles) in enumerate(waves):
            nxt = None
            if w + 1 < len(waves):
                npi, ntiles = waves[w + 1]
                nxt = scores(rhs[npi], ntiles, problems[npi][3])
            ss, m_w = cur
            m_new = m_w if state[pi] is None else jnp.maximum(state[pi][0], m_w)
            acc_w = l_w = None
            for kt, s in zip(tiles, ss):
                p = jnp.exp2(s - m_new)
                ps = jnp.sum(p, axis=0, keepdims=True)
                l_w = ps if l_w is None else l_w + ps
                v = vbuf[kt]
                if nxt is not None:
                    keep = (pltpu.bitcast(nxt[1][:, 0:T], jnp.int32) | 1) != 0
                    v = jnp.where(keep, v, jnp.zeros_like(v))
                d = _dot(v, p.astype(BF16))
                acc_w = d if acc_w is None else acc_w + d
            if state[pi] is None:
                state[pi] = (m_new, l_w, acc_w)
            else:
                alpha = jnp.exp2(state[pi][0] - m_new)
                state[pi] = (m_new, alpha * state[pi][1] + l_w, alpha * state[pi][2] + acc_w)
            cur = nxt
        for pr, (_, l, acc) in zip(problems, state):
            o = acc / l
            o = jnp.where(lo_half, o[:, 0:T], o[:, T:2 * T])
            o_ref[pr[4], 0] = (o * pr[2][0].astype(F32)).astype(BF16)

    for v in range(n // 2):
        @pl.when(j == v)
        def _(v=v):
            alongside()
            run([(qa_ref, qfa_ref, ga_ref, v + 1, 0), (qb_ref, qfb_ref, gb_ref, n - v, 1)])


def _sample_attn_parts(kpages, vpages, lpages, q_ref, tri_ref, scratch):
    qblk, gcarry, m_scr, l_scr, acc_scr = scratch
    P = len(kpages)
    sub = P // 2
    nq = q_ref.shape[1]
    R = LANES
    PG = LANES
    D = D_MODEL

    def begin(state):
        q = q_ref[0]
        rows = lax.broadcasted_iota(jnp.int32, (R, D), 0)
        cols = lax.broadcasted_iota(jnp.int32, (R, D), 1)
        tiled = jnp.concatenate([q] * (R // nq), axis=0)
        qblk[...] = jnp.where(cols // HEAD_DIM == rows // nq, tiled, jnp.zeros_like(tiled))
        if state is None:
            gcarry[...] = jnp.zeros_like(gcarry)
            m_scr[...] = jnp.full_like(m_scr, NEG_BIG)
            l_scr[...] = jnp.zeros_like(l_scr)
            acc_scr[...] = jnp.zeros_like(acc_scr)
        else:
            for dst, src in zip((gcarry, m_scr, l_scr, acc_scr), state):
                dst[...] = src[0]

    def save(state):
        for src, dst in zip((gcarry, m_scr, l_scr, acc_scr), state):
            dst[0] = src[...]

    def bias_rows(lf_pages):
        cs = _dot_exact_lhs(jnp.concatenate(lf_pages, axis=0), tri_ref[...])
        carry = gcarry[...]
        out = []
        for j in range(len(lf_pages)):
            G = cs[j * N_HEADS:(j + 1) * N_HEADS] + carry
            carry = jnp.broadcast_to(G[:, PG - 1:PG], carry.shape)
            out.append(jnp.concatenate(
                [jnp.broadcast_to(G[h:h + 1], (nq, PG)) for h in range(N_HEADS)], axis=0))
        gcarry[...] = carry
        return jnp.concatenate(out, axis=1) if len(out) > 1 else out[0]

    def update(s, vT):
        m_prev = m_scr[...]
        m_new = jnp.maximum(m_prev, jnp.max(s, axis=1, keepdims=True))
        alpha = jnp.exp2(m_prev - m_new)
        p = jnp.exp2(s - _tile_lanes(m_new, s.shape[1] // LANES))
        l_scr[...] = alpha * l_scr[...] + jnp.sum(p, axis=1, keepdims=True)
        acc_scr[...] = jnp.concatenate([alpha.T] * (D // LANES), axis=0) * acc_scr[...] + _dot_nt(vT, p.astype(BF16))
        m_scr[...] = m_new

    def chunk():
        bias = bias_rows([lpages[j][0] for j in range(P)]) * LOG2E
        subs = []
        for h0 in range(0, P, sub):
            kT = jnp.concatenate([kpages[j][0].astype(BF16) for j in range(h0, h0 + sub)], axis=1)
            subs.append(_dot(qblk[...], kT) - bias[:, h0 * PG:(h0 + sub) * PG])
        for h0, s in zip(range(0, P, sub), subs):
            update(s, jnp.concatenate([vpages[j][0].astype(BF16) for j in range(h0, h0 + sub)], axis=1))

    def finish(knew_ref, vnew_ref, lnew_ref, gate_ref, o_ref):
        s = _dot(qblk[...], knew_ref[0].astype(BF16)) - bias_rows([lnew_ref[0]]) * LOG2E
        key = lax.broadcasted_iota(jnp.int32, s.shape, 1)
        qi = lax.broadcasted_iota(jnp.int32, s.shape, 0) % nq
        update(jnp.where(key <= qi, s, NEG_BIG), vnew_ref[0].astype(BF16))

        linv = 1.0 / l_scr[...]
        oT = acc_scr[...] * jnp.concatenate([linv.T] * (D // LANES), axis=0)
        rows = lax.broadcasted_iota(jnp.int32, oT.shape, 0)
        cols = lax.broadcasted_iota(jnp.int32, oT.shape, 1)
        oT = jnp.where(rows // HEAD_DIM == cols // nq, oT, 0.0)
        shift = nq
        while shift < LANES:
            oT = oT + pltpu.roll(oT, shift, axis=1)
            shift *= 2
        o = oT.T[0:nq]
        o_ref[0] = (o * gate_ref[0].astype(F32)).astype(BF16)

    return begin, chunk, save, finish


N_PROMPT_IN = 9
N_STATE = 4


def _sample_scratch():
    return [pltpu.VMEM((LANES, D_MODEL), BF16),
            pltpu.VMEM((N_HEADS, LANES), F32),
            pltpu.VMEM((LANES, LANES), F32), pltpu.VMEM((LANES, LANES), F32),
            pltpu.VMEM((D_MODEL, LANES), F32)]


def _sample_state_shapes(nb):
    return [jax.ShapeDtypeStruct((nb, N_HEADS, LANES), F32), jax.ShapeDtypeStruct((nb, LANES, LANES), F32),
            jax.ShapeDtypeStruct((nb, LANES, LANES), F32), jax.ShapeDtypeStruct((nb, D_MODEL, LANES), F32)]


def _attn_kernel(pt_ref, *refs):
    del pt_ref
    P = ATT_PAGES_PER_STEP
    prompt_in = refs[:N_PROMPT_IN]
    rest = refs[N_PROMPT_IN:]
    kpages, vpages, lpages = rest[0:P], rest[P:2 * P], rest[2 * P:3 * P]
    knew_ref, vnew_ref, lnew_ref, q_ref, gate_ref, tri_ref = rest[3 * P:3 * P + 6]
    state_in = rest[3 * P + 6:3 * P + 6 + N_STATE]
    o_ref, os_ref, kbuf, vbuf = rest[3 * P + 6 + N_STATE:3 * P + 10 + N_STATE]
    scratch = rest[3 * P + 10 + N_STATE:]
    hp = pl.program_id(1)
    j = pl.program_id(2)
    nj = kbuf.shape[0] // 2
    n_chunks = (D_MODEL // LANES) * nj // SEQS_PER_BATCH_STEP
    c = (hp * nj + j) % n_chunks
    begin, chunk, _, finish = _sample_attn_parts(kpages, vpages, lpages, q_ref, tri_ref, scratch)
    pl.when(c == 0)(lambda: begin(state_in))
    _prompt_attn_step(hp, j, *prompt_in, o_ref, kbuf, vbuf, chunk)
    pl.when(c == n_chunks - 1)(lambda: finish(knew_ref, vnew_ref, lnew_ref, gate_ref, os_ref))


def _attention(qT, qfT, kT, kfT, vT, gateT, page_table, ckT, cvT, clT, knewT, vnewT, lnewT, qs, gates, tri_u,
               state):
    B, D, S = qT.shape
    NB, NQ, _ = qs.shape
    T = ATT_T
    n = S // T
    HP, NJ = D // LANES, n // 2
    P = ATT_PAGES_PER_STEP
    PG = ckT.shape[2]
    n_chunks = (page_table.shape[1] - PRE_PAGES) // P
    assert HP * NJ == SEQS_PER_BATCH_STEP * n_chunks and NB == B * SEQS_PER_BATCH_STEP
    assert PRE_PAGES + n_chunks * P == page_table.shape[1] and len(state) == N_STATE

    lo = lambda rows: pl.BlockSpec((1, LANES, T), lambda b, hp, j, pt: (b, hp if rows else 0, j))
    hi = lambda rows: pl.BlockSpec((1, LANES, T), lambda b, hp, j, pt: (b, hp if rows else 0, n - 1 - j))
    kspec = pl.BlockSpec((1, LANES, S), lambda b, hp, j, pt: (b, hp, 0))
    prompt_specs = [lo(True), hi(True), lo(False), hi(False),
                    kspec, pl.BlockSpec((1, LANES, S), lambda b, hp, j, pt: (b, 0, 0)), kspec, lo(True), hi(True)]

    seq = lambda b, hp, j: b * SEQS_PER_BATCH_STEP + (hp * NJ + j) // n_chunks

    def page_spec(rows, jj):
        return pl.BlockSpec(
            (1, rows, PG),
            lambda b, hp, j, pt: (pt[seq(b, hp, j), PRE_PAGES + ((hp * NJ + j) % n_chunks) * P + jj], 0, 0))

    per_seq = lambda rows, width: pl.BlockSpec((1, rows, width), lambda b, hp, j, pt: (seq(b, hp, j), 0, 0))
    sample_specs = ([page_spec(D, jj) for jj in range(P)] + [page_spec(D, jj) for jj in range(P)]
                    + [page_spec(N_HEADS, jj) for jj in range(P)]
                    + [per_seq(D, PG), per_seq(D, PG), per_seq(N_HEADS, PG), per_seq(NQ, D), per_seq(NQ, D),
                       pl.BlockSpec(tri_u.shape, lambda b, hp, j, pt: (0, 0))]
                    + [per_seq(*st.shape[1:]) for st in state])
    assert len(prompt_specs) == N_PROMPT_IN
    grid_spec = pltpu.PrefetchScalarGridSpec(
        num_scalar_prefetch=1,
        grid=(B, HP, NJ),
        in_specs=prompt_specs + sample_specs,
        out_specs=[pl.BlockSpec((2, 1, LANES, T), lambda b, hp, j, pt: (0, b, hp, j)), per_seq(NQ, D)],
        scratch_shapes=[pltpu.VMEM((n, T, 2 * LANES), BF16), pltpu.VMEM((n, LANES, T), BF16)] + _sample_scratch(),
    )
    return pl.pallas_call(
        _attn_kernel,
        grid_spec=grid_spec,
        out_shape=[jax.ShapeDtypeStruct((2, B, D, S // 2), BF16), jax.ShapeDtypeStruct((NB, NQ, D), BF16)],
        compiler_params=pltpu.CompilerParams(
            dimension_semantics=("arbitrary", "arbitrary", "arbitrary"), vmem_limit_bytes=VMEM_LIMIT),
        name="attn",
    )(page_table, qT, qT, qfT, qfT, kT, kfT, vT, gateT, gateT,
      *([ckT] * P), *([cvT] * P), *([clT] * P), knewT, vnewT, lnewT, qs, gates, tri_u, *state)


def _outproj_fm_kernel(ogT_ref, h_ref, w_ref, y_ref):
    T = ATT_T
    og = ogT_ref[0, 0]
    swapped = pl.program_id(1) >= pl.num_programs(1) // 2
    contract = lambda a: lax.dot_general(a, w_ref[...], (((0,), (0,)), ((), ())), preferred_element_type=F32)
    y_ref[0, 0:T] = h_ref[0, 0:T] + contract(jnp.where(swapped, og[:, T:2 * T], og[:, 0:T]))
    y_ref[0, T:2 * T] = h_ref[0, T:2 * T] + contract(jnp.where(swapped, og[:, 0:T], og[:, T:2 * T]))


def _outproj_fm(og2, h, w):
    _, B, D, S2 = og2.shape
    T = 2 * ATT_T
    n = 2 * S2 // T
    og_map = lambda b, t: (t // (n // 2), b, 0, jnp.where(t < n // 2, t, n - 1 - t))
    return pl.pallas_call(
        _outproj_fm_kernel,
        grid=(B, n),
        in_specs=[pl.BlockSpec((1, 1, D, T), og_map), pl.BlockSpec((1, T, D), lambda b, t: (b, t, 0)),
                  pl.BlockSpec(w.shape, lambda b, t: (0, 0), pipeline_mode=pl.Buffered(1))],
        out_specs=pl.BlockSpec((1, T, D), lambda b, t: (b, t, 0)),
        out_shape=jax.ShapeDtypeStruct((B, 2 * S2, D), F32),
        compiler_params=pltpu.CompilerParams(
            dimension_semantics=("arbitrary", "arbitrary"), vmem_limit_bytes=VMEM_LIMIT),
        name="outproj_prompt",
    )(og2, h, w)


def _outproj_kernel(og_ref, h_ref, w_ref, y_ref):
    y_ref[...] = h_ref[...] + _dot(og_ref[...], w_ref[...])


def _outproj(og, h, w):
    R, D = h.shape
    T = min(OUT_T, R)
    row = pl.BlockSpec((T, D), lambda i: (i, 0))
    return pl.pallas_call(
        _outproj_kernel,
        grid=(R // T,),
        in_specs=[row, row, pl.BlockSpec(w.shape, lambda i: (0, 0), pipeline_mode=pl.Buffered(1))],
        out_specs=row,
        out_shape=jax.ShapeDtypeStruct((R, D), F32),
        compiler_params=pltpu.CompilerParams(
            dimension_semantics=("arbitrary",), vmem_limit_bytes=VMEM_LIMIT),
        name="outproj",
    )(og, h, w)


def _upper_tri(n):
    return (jnp.arange(n)[:, None] <= jnp.arange(n)[None, :]).astype(BF16)


def kernel(x_prompt, x_sample, state_conv, cache_k, cache_v, cache_logf, page_table,
           a_norm, a_w_in, a_conv, a_w_out, kv_norm, kv_w, kv_fbias, k_norm,
           b_norm, b_w_in, q_norm, b_w_out):
    B, S, D = x_prompt.shape
    NB, NQ, _ = x_sample.shape
    n_pool, PG = cache_k.shape[0], cache_k.shape[1]
    H, hd = N_HEADS, HEAD_DIM
    assert a_norm.shape[0] == 1 and b_norm.shape[0] == 1, "one conv layer then one attention layer"
    assert D == D_MODEL and NQ == SUBLANES and PG == LANES

    common = [a_norm[0][None], a_w_in[0].astype(BF16), a_conv[0], a_w_out[0].astype(BF16),
              kv_norm[None], b_norm[0][None]]
    b_w_in_b = b_w_in[0].astype(BF16)
    kv_wT = kv_w.T
    kv_wT_ext = jnp.concatenate([kv_wT[:2 * D], jnp.repeat(kv_wT[2 * D:], AUG, axis=0), kv_wT[2 * D:]], axis=0)
    fbias_col = jnp.broadcast_to(jnp.concatenate([jnp.repeat(kv_fbias, AUG), kv_fbias])[:, None],
                                 (LANES + H, LANES))
    k_norm_col = jnp.broadcast_to(jnp.tile(k_norm, H)[:, None], (D, LANES))
    q_norm_col = jnp.broadcast_to(jnp.tile(q_norm[0], H)[:, None], (D, LANES))
    w_prompt = common + [kv_wT_ext.astype(BF16), fbias_col, k_norm_col, _upper_tri(PRE_T), b_w_in_b.T, q_norm_col]
    kv_w_row = jnp.concatenate([kv_w, jnp.zeros((D, LANES - H), kv_w.dtype)], axis=1)
    fbias_row = jnp.concatenate([kv_fbias, jnp.zeros((LANES - H,), kv_fbias.dtype)])[None]
    w_sample = common + [kv_w_row.astype(BF16), fbias_row, jnp.tile(k_norm, 2)[None], b_w_in_b,
                         jnp.tile(q_norm[0], 2)[None]]
    w_out_b = b_w_out[0].astype(BF16)

    st = state_conv[0]
    p1 = jnp.pad(st[:, 1:2], ((0, 0), (0, NQ - 1), (0, 0))).reshape(NB * NQ, D)
    p2 = jnp.pad(st, ((0, 0), (0, NQ - 2), (0, 0))).reshape(NB * NQ, D)
    (h1s, k_s, v_s, lf_s, qs, gate_s, cv_s) = _pre_sample(x_sample.reshape(NB * NQ, D), p1, p2, w_sample)
    qs = qs.reshape(NB, NQ, D)
    ckT = cache_k.transpose(0, 2, 3, 1).reshape(n_pool, D, PG)
    cvT = cache_v.transpose(0, 2, 3, 1).reshape(n_pool, D, PG)
    clT = cache_logf.transpose(0, 2, 1)
    tri_pg = _upper_tri(PG)
    (h1, kT, vT, lfT, qT, qfT, kfT, gateT, tail), sample_state = _pre_prompt(
        x_prompt, w_prompt, page_table, ckT, cvT, clT, qs, tri_pg)

    new_page = lambda a: jnp.pad(a.reshape(NB, NQ, -1), ((0, 0), (0, PG - NQ), (0, 0))).transpose(0, 2, 1)
    og2, ogs = _attention(
        qT, qfT, kT, kfT, vT, gateT, page_table, ckT, cvT, clT,
        new_page(k_s), new_page(v_s), new_page(lf_s),
        qs, gate_s.reshape(NB, NQ, D), tri_pg, sample_state)
    y_prompt = _outproj_fm(og2, h1, w_out_b)
    y_sample = _outproj(ogs.reshape(NB * NQ, D), h1s, w_out_b).reshape(NB, NQ, D)

    conv_prompt = tail[:, SUBLANES - (CONV_W - 1):][None]
    conv_sample = cv_s.reshape(NB, NQ, D)[:, NQ - (CONV_W - 1):][None]
    to_bshd = lambda t: t.reshape(B, H, hd, S).transpose(0, 3, 1, 2)
    return (y_prompt, y_sample, conv_prompt, conv_sample,
            to_bshd(kT), to_bshd(vT), lfT.transpose(0, 2, 1),
            k_s.reshape(NB, NQ, H, hd), v_s.reshape(NB, NQ, H, hd), lf_s.reshape(NB, NQ, H))
```

```python
import jax
import jax.numpy as jnp
from jax import lax
from jax.experimental import pallas as pl
from jax.experimental.pallas import tpu as pltpu

F32 = jnp.float32
BF16 = jnp.bfloat16

D_MODEL = 1024
N_HEADS = 16
HEAD_DIM = 64
CONV_W = 3
EPS = 1e-6
LOG2E = 1.4426950408889634
NEG_BIG = -1e30

LANES = 128
SUBLANES = 8
AUG = LANES // N_HEADS
VMEM_LIMIT = 56 * 1024 * 1024

PRE_T = 512
ATT_T = 256
ATT_WAVE = 3
OUT_T = 512
PAGES_PER_STEP = 8
SEQS_PER_BATCH_STEP = 4


def _dot(a, b):
    return jnp.dot(a, b, preferred_element_type=F32)


def _dot_nt(a, b):
    return lax.dot_general(a, b, (((1,), (1,)), ((), ())), preferred_element_type=F32)


def _tile_lanes(x, n):
    return jnp.concatenate([x] * n, axis=1) if n > 1 else x


def _split3(x):
    hi = x.astype(BF16)
    r1 = x - hi.astype(F32)
    mid = r1.astype(BF16)
    r2 = r1 - mid.astype(F32)
    return hi, mid, r2.astype(BF16)


def _dot_exact_rhs(m_bf16, x_f32):
    hi, mid, lo = _split3(x_f32)
    return _dot(m_bf16, hi) + _dot(m_bf16, mid) + _dot(m_bf16, lo)


def _dot_exact_lhs(x_f32, m_bf16):
    hi, mid, lo = _split3(x_f32)
    return _dot(hi, m_bf16) + _dot(mid, m_bf16) + _dot(lo, m_bf16)


def _rms(x, g):
    r = lax.rsqrt(jnp.mean(x * x, axis=-1, keepdims=True) + EPS)
    return x * r * g


def _head_rms(x, g128):
    lane = lax.broadcasted_iota(jnp.int32, (1, LANES), 1)
    lo_half = lane < HEAD_DIM
    outs = []
    for p in range(x.shape[1] // LANES):
        xp = x[:, p * LANES:(p + 1) * LANES]
        sq = xp * xp
        s_lo = jnp.sum(jnp.where(lo_half, sq, 0.0), axis=-1, keepdims=True)
        s_hi = jnp.sum(jnp.where(lo_half, 0.0, sq), axis=-1, keepdims=True)
        r = jnp.where(lo_half, lax.rsqrt(s_lo / HEAD_DIM + EPS), lax.rsqrt(s_hi / HEAD_DIM + EPS))
        outs.append(xp * r * g128)
    return jnp.concatenate(outs, axis=1)


def _head_rms_fm(xT, gT):
    n, T = xT.shape
    x3 = xT.reshape(n // HEAD_DIM, HEAD_DIM, T)
    r = lax.rsqrt(jnp.mean(x3 * x3, axis=1, keepdims=True) + EPS)
    return (x3 * r).reshape(n, T) * gT


def _silu(z):
    return z * jax.nn.sigmoid(z)


def _log_sigmoid(x):
    return jnp.minimum(x, 0.0) - jnp.log1p(jnp.exp(-jnp.abs(x)))


def _bias_select(F, j, key_side):
    hi, mid, lo = (t.astype(F32) for t in _split3(F))
    if key_side:
        out = jnp.where(j < 3, 1.0, jnp.where(j == 3, -hi, jnp.where(j == 4, -mid, jnp.where(j == 5, -lo, 0.0))))
    else:
        out = jnp.where(j == 0, hi, jnp.where(j == 1, mid, jnp.where(j == 2, lo, jnp.where(j < 6, 1.0, 0.0))))
    return out.astype(BF16)


def _conv_layer(x, prev1, prev2, m1, m2, w):
    D = D_MODEL
    xn = _rms(x, w['a_norm'][...]).astype(BF16)
    w_in = w['a_w_in']
    v = _dot(xn, w_in[:, D:2 * D]) * _dot(xn, w_in[:, 2 * D:3 * D])
    v1 = jnp.where(m1, prev1, pltpu.roll(v, 1, axis=0))
    v2 = jnp.where(m2, prev2, pltpu.roll(v, 2, axis=0))
    cw = w['a_conv'][...]
    conv = v2 * cw[0:1] + v1 * cw[1:2] + v * cw[2:3]
    b = _dot(xn, w_in[:, 0:D])
    z = _dot(xn, w_in[:, 3 * D:4 * D])
    g = (_silu(z) * b * conv).astype(BF16)
    return v, x + _dot(g, w['a_w_out'][...])


def _fox_proj(h1, w):
    D = D_MODEL
    xn = _rms(h1, w['b_norm'][...]).astype(BF16)
    b_w_in = w['b_w_in']
    q = _head_rms(_dot(xn, b_w_in[:, 0:D]), w['q_norm'][...]) * (LOG2E * HEAD_DIM ** -0.5)
    gate = _silu(_dot(xn, b_w_in[:, D:2 * D]))
    return q.astype(BF16), gate.astype(BF16)


_W_COMMON = ('a_norm', 'a_w_in', 'a_conv', 'a_w_out', 'kv_norm', 'b_norm')
_W_PROMPT = _W_COMMON + ('kv_wT', 'fbias_col', 'k_norm_col', 'tri_u', 'b_w_inT', 'q_norm_col')
_W_SAMPLE = _W_COMMON + ('kv_w', 'fbias_row', 'k_norm_row', 'b_w_in', 'q_norm')


def _pre_prompt_kernel(x_ref, *refs):
    nw = len(_W_PROMPT)
    w = dict(zip(_W_PROMPT, refs[:nw]))
    (h1_ref, kT_ref, vT_ref, lfT_ref, qT_ref, qfT_ref, kfT_ref, gateT_ref, tail_ref, vcarry, fcarry) = refs[nw:]
    T = x_ref.shape[1]
    D = D_MODEL
    reps = T // LANES

    @pl.when(pl.program_id(1) == 0)
    def _():
        vcarry[...] = jnp.zeros_like(vcarry)
        fcarry[...] = jnp.zeros_like(fcarry)

    row = lax.broadcasted_iota(jnp.int32, (T, 1), 0)
    prev = vcarry[...]
    pm1 = prev[SUBLANES - 1:SUBLANES]
    pm2 = prev[SUBLANES - 2:SUBLANES - 1]
    v, h1 = _conv_layer(x_ref[0], pm1, jnp.where(row == 0, pm2, pm1), row == 0, row < 2, w)
    tail = v[T - SUBLANES:T]
    vcarry[...] = tail
    tail_ref[0] = tail
    h1_ref[0] = h1

    hn = _rms(h1, w['kv_norm'][...]).astype(BF16)
    kvT = w['kv_wT']
    kT_ref[0] = _head_rms_fm(_dot_nt(kvT[0:D, :], hn), _tile_lanes(w['k_norm_col'][...], reps))
    vT_ref[0] = _dot_nt(kvT[D:2 * D, :], hn)
    fT = _dot_nt(kvT[2 * D:2 * D + LANES + N_HEADS, :], hn) + _tile_lanes(w['fbias_col'][...], reps)
    lfT = _log_sigmoid(fT)
    lfT_ref[0] = lfT[LANES:LANES + N_HEADS]
    FT = _dot_exact_lhs(lfT[0:LANES], w['tri_u'][...]) + _tile_lanes(fcarry[...], reps)
    fcarry[...] = jnp.broadcast_to(FT[:, T - 1:T], fcarry.shape)
    Fp = FT * LOG2E
    slot = lax.broadcasted_iota(jnp.int32, (LANES, 1), 0) % AUG
    kfT_ref[0] = _bias_select(Fp, slot, True)
    qfT_ref[0] = _bias_select(Fp, slot, False)

    xn2 = _rms(h1, w['b_norm'][...]).astype(BF16)
    b_w_inT = w['b_w_inT']
    qT = _head_rms_fm(_dot_nt(b_w_inT[0:D, :], xn2), _tile_lanes(w['q_norm_col'][...], reps))
    qT_ref[0] = (qT * (LOG2E * HEAD_DIM ** -0.5)).astype(BF16)
    gateT_ref[0] = _silu(_dot_nt(b_w_inT[D:2 * D, :], xn2)).astype(BF16)


def _pre_sample_kernel(x_ref, p1_ref, p2_ref, *refs):
    nw = len(_W_SAMPLE)
    w = dict(zip(_W_SAMPLE, refs[:nw]))
    h1_ref, k_ref, v_ref, lf_ref, q_ref, gate_ref, cv_ref = refs[nw:]
    D = D_MODEL
    t = lax.broadcasted_iota(jnp.int32, (x_ref.shape[0], 1), 0) % SUBLANES
    v, h1 = _conv_layer(x_ref[...], p1_ref[...], p2_ref[...], t == 0, t < 2, w)
    cv_ref[...] = v
    h1_ref[...] = h1
    hn = _rms(h1, w['kv_norm'][...]).astype(BF16)
    kv_w = w['kv_w']
    k_ref[...] = _head_rms(_dot(hn, kv_w[:, 0:D]), w['k_norm_row'][...])
    v_ref[...] = _dot(hn, kv_w[:, D:2 * D])
    lfe = _log_sigmoid(_dot(hn, kv_w[:, 2 * D:2 * D + LANES]) + w['fbias_row'][...])
    lf_ref[...] = lfe[:, 0:N_HEADS]
    q, gate = _fox_proj(h1, w)
    q_ref[...] = q
    gate_ref[...] = gate


def _pre_prompt(x, weights):
    B, S, D = x.shape
    T = PRE_T
    tok = lambda d: pl.BlockSpec((1, T, d), lambda b, j: (b, j, 0))
    feat = lambda n: pl.BlockSpec((1, n, T), lambda b, j: (b, 0, j))
    out_shape = [
        jax.ShapeDtypeStruct((B, S, D), F32),
        jax.ShapeDtypeStruct((B, D, S), F32),
        jax.ShapeDtypeStruct((B, D, S), F32),
        jax.ShapeDtypeStruct((B, N_HEADS, S), F32),
        jax.ShapeDtypeStruct((B, D, S), BF16),
        jax.ShapeDtypeStruct((B, LANES, S), BF16),
        jax.ShapeDtypeStruct((B, LANES, S), BF16),
        jax.ShapeDtypeStruct((B, D, S), BF16),
        jax.ShapeDtypeStruct((B, SUBLANES, D), F32),
    ]
    out_specs = [tok(D), feat(D), feat(D), feat(N_HEADS), feat(D), feat(LANES), feat(LANES), feat(D),
                 pl.BlockSpec((1, SUBLANES, D), lambda b, j: (b, 0, 0))]
    const = lambda b, j: (0, 0)
    w_specs = [pl.BlockSpec(w.shape, const, pipeline_mode=pl.Buffered(1)) for w in weights]
    return pl.pallas_call(
        _pre_prompt_kernel,
        grid=(B, S // T),
        in_specs=[tok(D)] + w_specs,
        out_specs=out_specs,
        out_shape=out_shape,
        scratch_shapes=[pltpu.VMEM((SUBLANES, D), F32), pltpu.VMEM((LANES, LANES), F32)],
        compiler_params=pltpu.CompilerParams(
            dimension_semantics=("arbitrary", "arbitrary"), vmem_limit_bytes=VMEM_LIMIT),
        name="pre_prompt",
    )(x, *weights)


def _pre_sample(x2, p1, p2, weights):
    R, D = x2.shape
    out_shape = [
        jax.ShapeDtypeStruct((R, D), F32),
        jax.ShapeDtypeStruct((R, D), F32),
        jax.ShapeDtypeStruct((R, D), F32),
        jax.ShapeDtypeStruct((R, N_HEADS), F32),
        jax.ShapeDtypeStruct((R, D), BF16),
        jax.ShapeDtypeStruct((R, D), BF16),
        jax.ShapeDtypeStruct((R, D), F32),
    ]
    return pl.pallas_call(
        _pre_sample_kernel,
        out_shape=out_shape,
        compiler_params=pltpu.CompilerParams(vmem_limit_bytes=VMEM_LIMIT),
        name="pre_sample",
    )(x2, p1, p2, *weights)


def _prompt_attn_step(hp, j, qa_ref, qb_ref, qfa_ref, qfb_ref, kT_ref, kfT_ref, vT_ref, ga_ref, gb_ref,
                      o_ref, kbuf, vbuf, alongside):
    T = qa_ref.shape[2]
    n = kbuf.shape[0]

    @pl.when(j == 0)
    def _():
        for kt in range(n):
            cols = slice(kt * T, (kt + 1) * T)
            kbuf[kt, :, 0:LANES] = kT_ref[0, :, cols].T.astype(BF16)
            kbuf[kt, :, LANES:2 * LANES] = kfT_ref[0, :, cols].astype(F32).T.astype(BF16)
            vbuf[kt] = vT_ref[0, :, cols].astype(BF16)

    row = lax.broadcasted_iota(jnp.int32, (LANES, 1), 0)
    lo_half = row < HEAD_DIM

    def query_operand(q_ref, qf_ref):
        q2 = q_ref[0]
        qf = qf_ref[0]
        zero = jnp.zeros_like(q2)
        return jnp.concatenate([
            jnp.concatenate([jnp.where(lo_half, q2, zero), jnp.where(row // AUG == 2 * hp, qf, zero)], axis=0),
            jnp.concatenate([jnp.where(lo_half, zero, q2), jnp.where(row // AUG == 2 * hp + 1, qf, zero)], axis=0),
        ], axis=1)

    def scores(rhs, tiles, n_tiles):
        ss = [_dot(kbuf[kt], rhs) for kt in tiles]
        if tiles[-1] == n_tiles - 1:
            r = lax.broadcasted_iota(jnp.int32, ss[-1].shape, 0)
            c = lax.broadcasted_iota(jnp.int32, ss[-1].shape, 1)
            ss[-1] = jnp.where(r <= jnp.where(c >= T, c - T, c), ss[-1], NEG_BIG)
        m = None
        for s in ss:
            ms = jnp.max(s, axis=0, keepdims=True)
            m = ms if m is None else jnp.maximum(m, ms)
        return ss, m

    def run(problems):
        waves = []
        for pi, pr in enumerate(problems):
            for c0 in range(0, pr[3], ATT_WAVE):
                waves.append((pi, list(range(c0, min(c0 + ATT_WAVE, pr[3])))))
        rhs = [query_operand(pr[0], pr[1]) for pr in problems]
        state = [None] * len(problems)
        cur = scores(rhs[waves[0][0]], waves[0][1], problems[waves[0][0]][3])
        for w, (pi, tiles) in enumerate(waves):
            nxt = None
            if w + 1 < len(waves):
                npi, ntiles = waves[w + 1]
                nxt = scores(rhs[npi], ntiles, problems[npi][3])
            ss, m_w = cur
            m_new = m_w if state[pi] is None else jnp.maximum(state[pi][0], m_w)
            acc_w = l_w = None
            for kt, s in zip(tiles, ss):
                p = jnp.exp2(s - m_new)
                ps = jnp.sum(p, axis=0, keepdims=True)
                l_w = ps if l_w is None else l_w + ps
                v = vbuf[kt]
                if nxt is not None:
                    keep = (pltpu.bitcast(nxt[1][:, 0:T], jnp.int32) | 1) != 0
                    v = jnp.where(keep, v, jnp.zeros_like(v))
                d = _dot(v, p.astype(BF16))
                acc_w = d if acc_w is None else acc_w + d
            if state[pi] is None:
                state[pi] = (m_new, l_w, acc_w)
            else:
                alpha = jnp.exp2(state[pi][0] - m_new)
                state[pi] = (m_new, alpha * state[pi][1] + l_w, alpha * state[pi][2] + acc_w)
            cur = nxt
        for pr, (_, l, acc) in zip(problems, state):
            o = acc / l
            o = jnp.where(lo_half, o[:, 0:T], o[:, T:2 * T])
            o_ref[pr[4], 0] = (o * pr[2][0].astype(F32)).astype(BF16)

    for v in range(n // 2):
        @pl.when(j == v)
        def _(v=v):
            alongside()
            run([(qa_ref, qfa_ref, ga_ref, v + 1, 0), (qb_ref, qfb_ref, gb_ref, n - v, 1)])


def _sample_attn_parts(refs):
    P = PAGES_PER_STEP
    sub = P // 2
    kpages = refs[0:P]
    vpages = refs[P:2 * P]
    lpages = refs[2 * P:3 * P]
    (knew_ref, vnew_ref, lnew_ref, q_ref, gate_ref, tri_ref,
     o_ref, qblk, gcarry, m_scr, l_scr, acc_scr) = refs[3 * P:]
    nq = q_ref.shape[1]
    R = LANES
    PG = LANES
    D = D_MODEL

    def begin():
        q = q_ref[0]
        rows = lax.broadcasted_iota(jnp.int32, (R, D), 0)
        cols = lax.broadcasted_iota(jnp.int32, (R, D), 1)
        tiled = jnp.concatenate([q] * (R // nq), axis=0)
        qblk[...] = jnp.where(cols // HEAD_DIM == rows // nq, tiled, jnp.zeros_like(tiled))
        gcarry[...] = jnp.zeros_like(gcarry)
        m_scr[...] = jnp.full_like(m_scr, NEG_BIG)
        l_scr[...] = jnp.zeros_like(l_scr)
        acc_scr[...] = jnp.zeros_like(acc_scr)

    def bias_rows(lf_pages):
        cs = _dot_exact_lhs(jnp.concatenate(lf_pages, axis=0), tri_ref[...])
        carry = gcarry[...]
        out = []
        for j in range(len(lf_pages)):
            G = cs[j * N_HEADS:(j + 1) * N_HEADS] + carry
            carry = jnp.broadcast_to(G[:, PG - 1:PG], carry.shape)
            out.append(jnp.concatenate(
                [jnp.broadcast_to(G[h:h + 1], (nq, PG)) for h in range(N_HEADS)], axis=0))
        gcarry[...] = carry
        return jnp.concatenate(out, axis=1) if len(out) > 1 else out[0]

    def update(s, vT):
        m_prev = m_scr[...]
        m_new = jnp.maximum(m_prev, jnp.max(s, axis=1, keepdims=True))
        alpha = jnp.exp2(m_prev - m_new)
        p = jnp.exp2(s - _tile_lanes(m_new, s.shape[1] // LANES))
        l_scr[...] = alpha * l_scr[...] + jnp.sum(p, axis=1, keepdims=True)
        acc_scr[...] = jnp.concatenate([alpha.T] * (D // LANES), axis=0) * acc_scr[...] + _dot_nt(vT, p.astype(BF16))
        m_scr[...] = m_new

    def chunk():
        bias = bias_rows([lpages[j][0] for j in range(P)]) * LOG2E
        subs = []
        for h0 in range(0, P, sub):
            kT = jnp.concatenate([kpages[j][0].astype(BF16) for j in range(h0, h0 + sub)], axis=1)
            subs.append(_dot(qblk[...], kT) - bias[:, h0 * PG:(h0 + sub) * PG])
        for h0, s in zip(range(0, P, sub), subs):
            update(s, jnp.concatenate([vpages[j][0].astype(BF16) for j in range(h0, h0 + sub)], axis=1))

    def finish():
        pad = jnp.zeros((PG - nq, D), F32)
        k_new = jnp.concatenate([knew_ref[0], pad], axis=0).astype(BF16)
        v_new = jnp.concatenate([vnew_ref[0], pad], axis=0).astype(BF16)
        s = _dot_nt(qblk[...], k_new) - bias_rows([lnew_ref[0]]) * LOG2E
        key = lax.broadcasted_iota(jnp.int32, s.shape, 1)
        qi = lax.broadcasted_iota(jnp.int32, s.shape, 0) % nq
        s = jnp.where(key <= qi, s, NEG_BIG)
        m_prev = m_scr[...]
        m_new = jnp.maximum(m_prev, jnp.max(s, axis=1, keepdims=True))
        alpha = jnp.exp2(m_prev - m_new)
        p = jnp.exp2(s - m_new)
        linv = 1.0 / (alpha * l_scr[...] + jnp.sum(p, axis=1, keepdims=True))

        oT = acc_scr[...] * jnp.concatenate([(alpha * linv).T] * (D // LANES), axis=0)
        rows = lax.broadcasted_iota(jnp.int32, oT.shape, 0)
        cols = lax.broadcasted_iota(jnp.int32, oT.shape, 1)
        oT = jnp.where(rows // HEAD_DIM == cols // nq, oT, 0.0)
        shift = nq
        while shift < LANES:
            oT = oT + pltpu.roll(oT, shift, axis=1)
            shift *= 2
        o = oT.T[0:nq]

        o2 = _dot(p.astype(BF16), v_new) * _tile_lanes(linv, D // LANES)
        r2 = lax.broadcasted_iota(jnp.int32, o2.shape, 0)
        c2 = lax.broadcasted_iota(jnp.int32, o2.shape, 1)
        o2 = jnp.where(c2 // HEAD_DIM == r2 // nq, o2, 0.0)
        for h in range(N_HEADS):
            o = o + o2[h * nq:(h + 1) * nq]
        o_ref[0] = (o * gate_ref[0].astype(F32)).astype(BF16)

    return begin, chunk, finish


N_PROMPT_IN = 9


def _attn_kernel(pt_ref, *refs):
    del pt_ref
    n_sample_in = 3 * PAGES_PER_STEP + 6
    prompt_in = refs[:N_PROMPT_IN]
    sample_in = refs[N_PROMPT_IN:N_PROMPT_IN + n_sample_in]
    o_ref, os_ref, kbuf, vbuf, qblk, gcarry, m_scr, l_scr, acc_scr = refs[N_PROMPT_IN + n_sample_in:]
    hp = pl.program_id(1)
    j = pl.program_id(2)
    nj = kbuf.shape[0] // 2
    n_chunks = (D_MODEL // LANES) * nj // SEQS_PER_BATCH_STEP
    c = (hp * nj + j) % n_chunks
    begin, chunk, finish = _sample_attn_parts(sample_in + (os_ref, qblk, gcarry, m_scr, l_scr, acc_scr))
    pl.when(c == 0)(begin)
    _prompt_attn_step(hp, j, *prompt_in, o_ref, kbuf, vbuf, chunk)
    pl.when(c == n_chunks - 1)(finish)


def _attention(qT, qfT, kT, kfT, vT, gateT, page_table, ckT, cvT, clT, knew, vnew, lnewT, qs, gates, tri_u):
    B, D, S = qT.shape
    NB, NQ, _ = qs.shape
    T = ATT_T
    n = S // T
    HP, NJ = D // LANES, n // 2
    P = PAGES_PER_STEP
    PG = ckT.shape[2]
    n_chunks = page_table.shape[1] // P
    assert HP * NJ == SEQS_PER_BATCH_STEP * n_chunks and NB == B * SEQS_PER_BATCH_STEP

    lo = lambda rows: pl.BlockSpec((1, LANES, T), lambda b, hp, j, pt: (b, hp if rows else 0, j))
    hi = lambda rows: pl.BlockSpec((1, LANES, T), lambda b, hp, j, pt: (b, hp if rows else 0, n - 1 - j))
    kspec = pl.BlockSpec((1, LANES, S), lambda b, hp, j, pt: (b, hp, 0))
    prompt_specs = [lo(True), hi(True), lo(False), hi(False),
                    kspec, pl.BlockSpec((1, LANES, S), lambda b, hp, j, pt: (b, 0, 0)), kspec, lo(True), hi(True)]

    seq = lambda b, hp, j: b * SEQS_PER_BATCH_STEP + (hp * NJ + j) // n_chunks

    def page_spec(rows, jj):
        return pl.BlockSpec(
            (1, rows, PG),
            lambda b, hp, j, pt: (pt[seq(b, hp, j), ((hp * NJ + j) % n_chunks) * P + jj], 0, 0))

    per_seq = lambda rows, width: pl.BlockSpec((1, rows, width), lambda b, hp, j, pt: (seq(b, hp, j), 0, 0))
    sample_specs = ([page_spec(D, jj) for jj in range(P)] + [page_spec(D, jj) for jj in range(P)]
                    + [page_spec(N_HEADS, jj) for jj in range(P)]
                    + [per_seq(NQ, D), per_seq(NQ, D), per_seq(N_HEADS, PG), per_seq(NQ, D), per_seq(NQ, D),
                       pl.BlockSpec(tri_u.shape, lambda b, hp, j, pt: (0, 0))])
    assert len(prompt_specs) == N_PROMPT_IN
    grid_spec = pltpu.PrefetchScalarGridSpec(
        num_scalar_prefetch=1,
        grid=(B, HP, NJ),
        in_specs=prompt_specs + sample_specs,
        out_specs=[pl.BlockSpec((2, 1, LANES, T), lambda b, hp, j, pt: (0, b, hp, j)), per_seq(NQ, D)],
        scratch_shapes=[pltpu.VMEM((n, T, 2 * LANES), BF16), pltpu.VMEM((n, LANES, T), BF16),
                        pltpu.VMEM((LANES, D), BF16),
                        pltpu.VMEM((N_HEADS, LANES), F32),
                        pltpu.VMEM((LANES, LANES), F32), pltpu.VMEM((LANES, LANES), F32),
                        pltpu.VMEM((D, LANES), F32)],
    )
    return pl.pallas_call(
        _attn_kernel,
        grid_spec=grid_spec,
        out_shape=[jax.ShapeDtypeStruct((2, B, D, S // 2), BF16), jax.ShapeDtypeStruct((NB, NQ, D), BF16)],
        compiler_params=pltpu.CompilerParams(
            dimension_semantics=("arbitrary", "arbitrary", "arbitrary"), vmem_limit_bytes=VMEM_LIMIT),
        name="attn",
    )(page_table, qT, qT, qfT, qfT, kT, kfT, vT, gateT, gateT,
      *([ckT] * P), *([cvT] * P), *([clT] * P), knew, vnew, lnewT, qs, gates, tri_u)


def _outproj_fm_kernel(ogT_ref, h_ref, w_ref, y_ref):
    T = ATT_T
    og = ogT_ref[0, 0]
    swapped = pl.program_id(1) >= pl.num_programs(1) // 2
    contract = lambda a: lax.dot_general(a, w_ref[...], (((0,), (0,)), ((), ())), preferred_element_type=F32)
    y_ref[0, 0:T] = h_ref[0, 0:T] + contract(jnp.where(swapped, og[:, T:2 * T], og[:, 0:T]))
    y_ref[0, T:2 * T] = h_ref[0, T:2 * T] + contract(jnp.where(swapped, og[:, 0:T], og[:, T:2 * T]))


def _outproj_fm(og2, h, w):
    _, B, D, S2 = og2.shape
    T = 2 * ATT_T
    n = 2 * S2 // T
    og_map = lambda b, t: (t // (n // 2), b, 0, jnp.where(t < n // 2, t, n - 1 - t))
    return pl.pallas_call(
        _outproj_fm_kernel,
        grid=(B, n),
        in_specs=[pl.BlockSpec((1, 1, D, T), og_map), pl.BlockSpec((1, T, D), lambda b, t: (b, t, 0)),
                  pl.BlockSpec(w.shape, lambda b, t: (0, 0), pipeline_mode=pl.Buffered(1))],
        out_specs=pl.BlockSpec((1, T, D), lambda b, t: (b, t, 0)),
        out_shape=jax.ShapeDtypeStruct((B, 2 * S2, D), F32),
        compiler_params=pltpu.CompilerParams(
            dimension_semantics=("arbitrary", "arbitrary"), vmem_limit_bytes=VMEM_LIMIT),
        name="outproj_prompt",
    )(og2, h, w)


def _outproj_kernel(og_ref, h_ref, w_ref, y_ref):
    y_ref[...] = h_ref[...] + _dot(og_ref[...], w_ref[...])


def _outproj(og, h, w):
    R, D = h.shape
    T = min(OUT_T, R)
    row = pl.BlockSpec((T, D), lambda i: (i, 0))
    return pl.pallas_call(
        _outproj_kernel,
        grid=(R // T,),
        in_specs=[row, row, pl.BlockSpec(w.shape, lambda i: (0, 0), pipeline_mode=pl.Buffered(1))],
        out_specs=row,
        out_shape=jax.ShapeDtypeStruct((R, D), F32),
        compiler_params=pltpu.CompilerParams(
            dimension_semantics=("arbitrary",), vmem_limit_bytes=VMEM_LIMIT),
        name="outproj",
    )(og, h, w)


def _upper_tri(n):
    return (jnp.arange(n)[:, None] <= jnp.arange(n)[None, :]).astype(BF16)


def kernel(x_prompt, x_sample, state_conv, cache_k, cache_v, cache_logf, page_table,
           a_norm, a_w_in, a_conv, a_w_out, kv_norm, kv_w, kv_fbias, k_norm,
           b_norm, b_w_in, q_norm, b_w_out):
    B, S, D = x_prompt.shape
    NB, NQ, _ = x_sample.shape
    n_pool, PG = cache_k.shape[0], cache_k.shape[1]
    H, hd = N_HEADS, HEAD_DIM
    assert a_norm.shape[0] == 1 and b_norm.shape[0] == 1, "one conv layer then one attention layer"
    assert D == D_MODEL and NQ == SUBLANES and PG == LANES

    common = [a_norm[0][None], a_w_in[0].astype(BF16), a_conv[0], a_w_out[0].astype(BF16),
              kv_norm[None], b_norm[0][None]]
    b_w_in_b = b_w_in[0].astype(BF16)
    kv_wT = kv_w.T
    kv_wT_ext = jnp.concatenate([kv_wT[:2 * D], jnp.repeat(kv_wT[2 * D:], AUG, axis=0), kv_wT[2 * D:]], axis=0)
    fbias_col = jnp.broadcast_to(jnp.concatenate([jnp.repeat(kv_fbias, AUG), kv_fbias])[:, None],
                                 (LANES + H, LANES))
    k_norm_col = jnp.broadcast_to(jnp.tile(k_norm, H)[:, None], (D, LANES))
    q_norm_col = jnp.broadcast_to(jnp.tile(q_norm[0], H)[:, None], (D, LANES))
    w_prompt = common + [kv_wT_ext.astype(BF16), fbias_col, k_norm_col, _upper_tri(PRE_T), b_w_in_b.T, q_norm_col]
    kv_w_row = jnp.concatenate([kv_w, jnp.zeros((D, LANES - H), kv_w.dtype)], axis=1)
    fbias_row = jnp.concatenate([kv_fbias, jnp.zeros((LANES - H,), kv_fbias.dtype)])[None]
    w_sample = common + [kv_w_row.astype(BF16), fbias_row, jnp.tile(k_norm, 2)[None], b_w_in_b,
                         jnp.tile(q_norm[0], 2)[None]]
    w_out_b = b_w_out[0].astype(BF16)

    (h1, kT, vT, lfT, qT, qfT, kfT, gateT, tail) = _pre_prompt(x_prompt, w_prompt)
    st = state_conv[0]
    p1 = jnp.pad(st[:, 1:2], ((0, 0), (0, NQ - 1), (0, 0))).reshape(NB * NQ, D)
    p2 = jnp.pad(st, ((0, 0), (0, NQ - 2), (0, 0))).reshape(NB * NQ, D)
    (h1s, k_s, v_s, lf_s, qs, gate_s, cv_s) = _pre_sample(x_sample.reshape(NB * NQ, D), p1, p2, w_sample)

    lnewT = jnp.pad(lf_s.reshape(NB, NQ, H), ((0, 0), (0, PG - NQ), (0, 0))).transpose(0, 2, 1)
    og2, ogs = _attention(
        qT, qfT, kT, kfT, vT, gateT, page_table,
        cache_k.transpose(0, 2, 3, 1).reshape(n_pool, D, PG),
        cache_v.transpose(0, 2, 3, 1).reshape(n_pool, D, PG),
        cache_logf.transpose(0, 2, 1),
        k_s.reshape(NB, NQ, D), v_s.reshape(NB, NQ, D), lnewT,
        qs.reshape(NB, NQ, D), gate_s.reshape(NB, NQ, D), _upper_tri(PG))
    y_prompt = _outproj_fm(og2, h1, w_out_b)
    y_sample = _outproj(ogs.reshape(NB * NQ, D), h1s, w_out_b).reshape(NB, NQ, D)

    conv_prompt = tail[:, SUBLANES - (CONV_W - 1):][None]
    conv_sample = cv_s.reshape(NB, NQ, D)[:, NQ - (CONV_W - 1):][None]
    to_bshd = lambda t: t.reshape(B, H, hd, S).transpose(0, 3, 1, 2)
    return (y_prompt, y_sample, conv_prompt, conv_sample,
            to_bshd(kT), to_bshd(vT), lfT.transpose(0, 2, 1),
            k_s.reshape(NB, NQ, H, hd), v_s.reshape(NB, NQ, H, hd), lf_s.reshape(NB, NQ, H))
```

```python
import jax
import jax.numpy as jnp
from jax import lax
from jax.experimental import pallas as pl
from jax.experimental.pallas import tpu as pltpu

F32 = jnp.float32
BF16 = jnp.bfloat16

D_MODEL = 1024
N_HEADS = 16
HEAD_DIM = 64
CONV_W = 3
EPS = 1e-6
LOG2E = 1.4426950408889634
NEG_BIG = -1e30

LANES = 128
SUBLANES = 8
AUG = LANES // N_HEADS
VMEM_LIMIT = 56 * 1024 * 1024

PRE_T = 512
ATT_T = 256
OUT_T = 512
PAGES_PER_STEP = 8
SEQS_PER_BATCH_STEP = 4


def _dot(a, b):
    return jnp.dot(a, b, preferred_element_type=F32)


def _dot_nt(a, b):
    return lax.dot_general(a, b, (((1,), (1,)), ((), ())), preferred_element_type=F32)


def _tile_lanes(x, n):
    return jnp.concatenate([x] * n, axis=1) if n > 1 else x


def _split3(x):
    hi = x.astype(BF16)
    r1 = x - hi.astype(F32)
    mid = r1.astype(BF16)
    r2 = r1 - mid.astype(F32)
    return hi, mid, r2.astype(BF16)


def _dot_exact_rhs(m_bf16, x_f32):
    hi, mid, lo = _split3(x_f32)
    return _dot(m_bf16, hi) + _dot(m_bf16, mid) + _dot(m_bf16, lo)


def _dot_exact_lhs(x_f32, m_bf16):
    hi, mid, lo = _split3(x_f32)
    return _dot(hi, m_bf16) + _dot(mid, m_bf16) + _dot(lo, m_bf16)


def _rms(x, g):
    r = lax.rsqrt(jnp.mean(x * x, axis=-1, keepdims=True) + EPS)
    return x * r * g


def _head_rms(x, g128):
    lane = lax.broadcasted_iota(jnp.int32, (1, LANES), 1)
    lo_half = lane < HEAD_DIM
    outs = []
    for p in range(x.shape[1] // LANES):
        xp = x[:, p * LANES:(p + 1) * LANES]
        sq = xp * xp
        s_lo = jnp.sum(jnp.where(lo_half, sq, 0.0), axis=-1, keepdims=True)
        s_hi = jnp.sum(jnp.where(lo_half, 0.0, sq), axis=-1, keepdims=True)
        r = jnp.where(lo_half, lax.rsqrt(s_lo / HEAD_DIM + EPS), lax.rsqrt(s_hi / HEAD_DIM + EPS))
        outs.append(xp * r * g128)
    return jnp.concatenate(outs, axis=1)


def _head_rms_fm(xT, gT):
    n, T = xT.shape
    x3 = xT.reshape(n // HEAD_DIM, HEAD_DIM, T)
    r = lax.rsqrt(jnp.mean(x3 * x3, axis=1, keepdims=True) + EPS)
    return (x3 * r).reshape(n, T) * gT


def _silu(z):
    return z * jax.nn.sigmoid(z)


def _log_sigmoid(x):
    return jnp.minimum(x, 0.0) - jnp.log1p(jnp.exp(-jnp.abs(x)))


def _bias_select(F, j, key_side):
    hi, mid, lo = (t.astype(F32) for t in _split3(F))
    if key_side:
        out = jnp.where(j < 3, 1.0, jnp.where(j == 3, -hi, jnp.where(j == 4, -mid, jnp.where(j == 5, -lo, 0.0))))
    else:
        out = jnp.where(j == 0, hi, jnp.where(j == 1, mid, jnp.where(j == 2, lo, jnp.where(j < 6, 1.0, 0.0))))
    return out.astype(BF16)


def _conv_layer(x, prev1, prev2, m1, m2, w):
    D = D_MODEL
    xn = _rms(x, w['a_norm'][...]).astype(BF16)
    w_in = w['a_w_in']
    v = _dot(xn, w_in[:, D:2 * D]) * _dot(xn, w_in[:, 2 * D:3 * D])
    v1 = jnp.where(m1, prev1, pltpu.roll(v, 1, axis=0))
    v2 = jnp.where(m2, prev2, pltpu.roll(v, 2, axis=0))
    cw = w['a_conv'][...]
    conv = v2 * cw[0:1] + v1 * cw[1:2] + v * cw[2:3]
    b = _dot(xn, w_in[:, 0:D])
    z = _dot(xn, w_in[:, 3 * D:4 * D])
    g = (_silu(z) * b * conv).astype(BF16)
    return v, x + _dot(g, w['a_w_out'][...])


def _fox_proj(h1, w):
    D = D_MODEL
    xn = _rms(h1, w['b_norm'][...]).astype(BF16)
    b_w_in = w['b_w_in']
    q = _head_rms(_dot(xn, b_w_in[:, 0:D]), w['q_norm'][...]) * (LOG2E * HEAD_DIM ** -0.5)
    gate = _silu(_dot(xn, b_w_in[:, D:2 * D]))
    return q.astype(BF16), gate.astype(BF16)


_W_COMMON = ('a_norm', 'a_w_in', 'a_conv', 'a_w_out', 'kv_norm', 'b_norm')
_W_PROMPT = _W_COMMON + ('kv_wT', 'fbias_col', 'k_norm_col', 'tri_u', 'b_w_inT', 'q_norm_col')
_W_SAMPLE = _W_COMMON + ('kv_wT', 'fbias_row', 'k_norm_row', 'b_w_in', 'q_norm')


def _pre_prompt_kernel(x_ref, *refs):
    nw = len(_W_PROMPT)
    w = dict(zip(_W_PROMPT, refs[:nw]))
    (h1_ref, kT_ref, vT_ref, lfT_ref, qT_ref, qfT_ref, kfT_ref, gateT_ref, tail_ref, vcarry, fcarry) = refs[nw:]
    T = x_ref.shape[1]
    D = D_MODEL
    reps = T // LANES

    @pl.when(pl.program_id(1) == 0)
    def _():
        vcarry[...] = jnp.zeros_like(vcarry)
        fcarry[...] = jnp.zeros_like(fcarry)

    row = lax.broadcasted_iota(jnp.int32, (T, 1), 0)
    prev = vcarry[...]
    pm1 = prev[SUBLANES - 1:SUBLANES]
    pm2 = prev[SUBLANES - 2:SUBLANES - 1]
    v, h1 = _conv_layer(x_ref[0], pm1, jnp.where(row == 0, pm2, pm1), row == 0, row < 2, w)
    tail = v[T - SUBLANES:T]
    vcarry[...] = tail
    tail_ref[0] = tail
    h1_ref[0] = h1

    hn = _rms(h1, w['kv_norm'][...]).astype(BF16)
    kvT = w['kv_wT']
    kT_ref[0] = _head_rms_fm(_dot_nt(kvT[0:D, :], hn), _tile_lanes(w['k_norm_col'][...], reps))
    vT_ref[0] = _dot_nt(kvT[D:2 * D, :], hn)
    fT = _dot_nt(kvT[2 * D:2 * D + LANES + N_HEADS, :], hn) + _tile_lanes(w['fbias_col'][...], reps)
    lfT = _log_sigmoid(fT)
    lfT_ref[0] = lfT[LANES:LANES + N_HEADS]
    FT = _dot_exact_lhs(lfT[0:LANES], w['tri_u'][...]) + _tile_lanes(fcarry[...], reps)
    fcarry[...] = jnp.broadcast_to(FT[:, T - 1:T], fcarry.shape)
    Fp = FT * LOG2E
    slot = lax.broadcasted_iota(jnp.int32, (LANES, 1), 0) % AUG
    kfT_ref[0] = _bias_select(Fp, slot, True)
    qfT_ref[0] = _bias_select(Fp, slot, False)

    xn2 = _rms(h1, w['b_norm'][...]).astype(BF16)
    b_w_inT = w['b_w_inT']
    qT = _head_rms_fm(_dot_nt(b_w_inT[0:D, :], xn2), _tile_lanes(w['q_norm_col'][...], reps))
    qT_ref[0] = (qT * (LOG2E * HEAD_DIM ** -0.5)).astype(BF16)
    gateT_ref[0] = _silu(_dot_nt(b_w_inT[D:2 * D, :], xn2)).astype(BF16)


def _pre_sample_kernel(x_ref, p1_ref, p2_ref, *refs):
    nw = len(_W_SAMPLE)
    w = dict(zip(_W_SAMPLE, refs[:nw]))
    h1_ref, k_ref, v_ref, lf_ref, q_ref, gate_ref, cv_ref = refs[nw:]
    D = D_MODEL
    t = lax.broadcasted_iota(jnp.int32, (x_ref.shape[0], 1), 0) % SUBLANES
    v, h1 = _conv_layer(x_ref[...], p1_ref[...], p2_ref[...], t == 0, t < 2, w)
    cv_ref[...] = v
    h1_ref[...] = h1
    hn = _rms(h1, w['kv_norm'][...]).astype(BF16)
    kvT = w['kv_wT']
    k_ref[...] = _head_rms(_dot_nt(hn, kvT[0:D, :]), w['k_norm_row'][...])
    v_ref[...] = _dot_nt(hn, kvT[D:2 * D, :])
    lf_ref[...] = _log_sigmoid(_dot_nt(hn, kvT[2 * D + LANES:2 * D + LANES + N_HEADS, :]) + w['fbias_row'][...])
    q, gate = _fox_proj(h1, w)
    q_ref[...] = q
    gate_ref[...] = gate


def _pre_prompt(x, weights):
    B, S, D = x.shape
    T = PRE_T
    tok = lambda d: pl.BlockSpec((1, T, d), lambda b, j: (b, j, 0))
    feat = lambda n: pl.BlockSpec((1, n, T), lambda b, j: (b, 0, j))
    out_shape = [
        jax.ShapeDtypeStruct((B, S, D), F32),
        jax.ShapeDtypeStruct((B, D, S), F32),
        jax.ShapeDtypeStruct((B, D, S), F32),
        jax.ShapeDtypeStruct((B, N_HEADS, S), F32),
        jax.ShapeDtypeStruct((B, D, S), BF16),
        jax.ShapeDtypeStruct((B, LANES, S), BF16),
        jax.ShapeDtypeStruct((B, LANES, S), BF16),
        jax.ShapeDtypeStruct((B, D, S), BF16),
        jax.ShapeDtypeStruct((B, SUBLANES, D), F32),
    ]
    out_specs = [tok(D), feat(D), feat(D), feat(N_HEADS), feat(D), feat(LANES), feat(LANES), feat(D),
                 pl.BlockSpec((1, SUBLANES, D), lambda b, j: (b, 0, 0))]
    const = lambda b, j: (0, 0)
    w_specs = [pl.BlockSpec(w.shape, const, pipeline_mode=pl.Buffered(1)) for w in weights]
    return pl.pallas_call(
        _pre_prompt_kernel,
        grid=(B, S // T),
        in_specs=[tok(D)] + w_specs,
        out_specs=out_specs,
        out_shape=out_shape,
        scratch_shapes=[pltpu.VMEM((SUBLANES, D), F32), pltpu.VMEM((LANES, LANES), F32)],
        compiler_params=pltpu.CompilerParams(
            dimension_semantics=("arbitrary", "arbitrary"), vmem_limit_bytes=VMEM_LIMIT),
        name="pre_prompt",
    )(x, *weights)


def _pre_sample(x2, p1, p2, weights):
    R, D = x2.shape
    out_shape = [
        jax.ShapeDtypeStruct((R, D), F32),
        jax.ShapeDtypeStruct((R, D), F32),
        jax.ShapeDtypeStruct((R, D), F32),
        jax.ShapeDtypeStruct((R, N_HEADS), F32),
        jax.ShapeDtypeStruct((R, D), BF16),
        jax.ShapeDtypeStruct((R, D), BF16),
        jax.ShapeDtypeStruct((R, D), F32),
    ]
    return pl.pallas_call(
        _pre_sample_kernel,
        out_shape=out_shape,
        compiler_params=pltpu.CompilerParams(vmem_limit_bytes=VMEM_LIMIT),
        name="pre_sample",
    )(x2, p1, p2, *weights)


def _prompt_attn_step(hp, j, qa_ref, qb_ref, qfa_ref, qfb_ref, kT_ref, kfT_ref, vT_ref, ga_ref, gb_ref,
                      o_ref, kbuf, vbuf, alongside):
    T = qa_ref.shape[2]
    n = kbuf.shape[0]

    @pl.when(j == 0)
    def _():
        for kt in range(n):
            cols = slice(kt * T, (kt + 1) * T)
            kbuf[kt, :, 0:LANES] = kT_ref[0, :, cols].T.astype(BF16)
            kbuf[kt, :, LANES:2 * LANES] = kfT_ref[0, :, cols].astype(F32).T.astype(BF16)
            vbuf[kt] = vT_ref[0, :, cols].astype(BF16)

    row = lax.broadcasted_iota(jnp.int32, (LANES, 1), 0)
    lo_half = row < HEAD_DIM

    def query_operand(q_ref, qf_ref):
        q2 = q_ref[0]
        qf = qf_ref[0]
        zero = jnp.zeros_like(q2)
        return jnp.concatenate([
            jnp.concatenate([jnp.where(lo_half, q2, zero), jnp.where(row // AUG == 2 * hp, qf, zero)], axis=0),
            jnp.concatenate([jnp.where(lo_half, zero, q2), jnp.where(row // AUG == 2 * hp + 1, qf, zero)], axis=0),
        ], axis=1)

    def attend(q_ref, qf_ref, g_ref, n_tiles, half):
        rhs = query_operand(q_ref, qf_ref)
        ss = [_dot(kbuf[kt], rhs) for kt in range(n_tiles)]
        r = lax.broadcasted_iota(jnp.int32, ss[-1].shape, 0)
        c = lax.broadcasted_iota(jnp.int32, ss[-1].shape, 1)
        ss[-1] = jnp.where(r <= jnp.where(c >= T, c - T, c), ss[-1], NEG_BIG)
        m = None
        for s in ss:
            ms = jnp.max(s, axis=0, keepdims=True)
            m = ms if m is None else jnp.maximum(m, ms)
        acc = l = None
        for kt, s in enumerate(ss):
            p = jnp.exp2(s - m)
            ps = jnp.sum(p, axis=0, keepdims=True)
            l = ps if l is None else l + ps
            d = _dot(vbuf[kt], p.astype(BF16))
            acc = d if acc is None else acc + d
        o = acc / l
        o = jnp.where(lo_half, o[:, 0:T], o[:, T:2 * T])
        o_ref[half, 0] = (o * g_ref[0].astype(F32)).astype(BF16)

    for v in range(n // 2):
        @pl.when(j == v)
        def _(v=v):
            alongside()
            attend(qa_ref, qfa_ref, ga_ref, v + 1, 0)
            attend(qb_ref, qfb_ref, gb_ref, n - v, 1)


def _sample_attn_parts(refs):
    P = PAGES_PER_STEP
    sub = P // 2
    kpages = refs[0:P]
    vpages = refs[P:2 * P]
    lpages = refs[2 * P:3 * P]
    (knew_ref, vnew_ref, lnew_ref, q_ref, gate_ref, tri_ref,
     o_ref, qblk, gcarry, m_scr, l_scr, acc_scr) = refs[3 * P:]
    nq = q_ref.shape[1]
    R = LANES
    PG = LANES
    D = D_MODEL

    def begin():
        q = q_ref[0]
        rows = lax.broadcasted_iota(jnp.int32, (R, D), 0)
        cols = lax.broadcasted_iota(jnp.int32, (R, D), 1)
        tiled = jnp.concatenate([q] * (R // nq), axis=0)
        qblk[...] = jnp.where(cols // HEAD_DIM == rows // nq, tiled, jnp.zeros_like(tiled))
        gcarry[...] = jnp.zeros_like(gcarry)
        m_scr[...] = jnp.full_like(m_scr, NEG_BIG)
        l_scr[...] = jnp.zeros_like(l_scr)
        acc_scr[...] = jnp.zeros_like(acc_scr)

    def bias_rows(lf_pages):
        cs = _dot_exact_lhs(jnp.concatenate(lf_pages, axis=0), tri_ref[...])
        carry = gcarry[...]
        out = []
        for j in range(len(lf_pages)):
            G = cs[j * N_HEADS:(j + 1) * N_HEADS] + carry
            carry = jnp.broadcast_to(G[:, PG - 1:PG], carry.shape)
            out.append(jnp.concatenate(
                [jnp.broadcast_to(G[h:h + 1], (nq, PG)) for h in range(N_HEADS)], axis=0))
        gcarry[...] = carry
        return jnp.concatenate(out, axis=1) if len(out) > 1 else out[0]

    def update(s, vT):
        m_prev = m_scr[...]
        m_new = jnp.maximum(m_prev, jnp.max(s, axis=1, keepdims=True))
        alpha = jnp.exp2(m_prev - m_new)
        p = jnp.exp2(s - _tile_lanes(m_new, s.shape[1] // LANES))
        l_scr[...] = alpha * l_scr[...] + jnp.sum(p, axis=1, keepdims=True)
        acc_scr[...] = jnp.concatenate([alpha.T] * (D // LANES), axis=0) * acc_scr[...] + _dot_nt(vT, p.astype(BF16))
        m_scr[...] = m_new

    def chunk():
        bias = bias_rows([lpages[j][0] for j in range(P)]) * LOG2E
        subs = []
        for h0 in range(0, P, sub):
            kT = jnp.concatenate([kpages[j][0].astype(BF16) for j in range(h0, h0 + sub)], axis=1)
            subs.append(_dot(qblk[...], kT) - bias[:, h0 * PG:(h0 + sub) * PG])
        for h0, s in zip(range(0, P, sub), subs):
            update(s, jnp.concatenate([vpages[j][0].astype(BF16) for j in range(h0, h0 + sub)], axis=1))

    def finish():
        pad = jnp.zeros((PG - nq, D), F32)
        k_new = jnp.concatenate([knew_ref[0], pad], axis=0).astype(BF16)
        v_new = jnp.concatenate([vnew_ref[0], pad], axis=0).astype(BF16)
        s = _dot_nt(qblk[...], k_new) - bias_rows([lnew_ref[0]]) * LOG2E
        key = lax.broadcasted_iota(jnp.int32, s.shape, 1)
        qi = lax.broadcasted_iota(jnp.int32, s.shape, 0) % nq
        s = jnp.where(key <= qi, s, NEG_BIG)
        m_prev = m_scr[...]
        m_new = jnp.maximum(m_prev, jnp.max(s, axis=1, keepdims=True))
        alpha = jnp.exp2(m_prev - m_new)
        p = jnp.exp2(s - m_new)
        linv = 1.0 / (alpha * l_scr[...] + jnp.sum(p, axis=1, keepdims=True))

        oT = acc_scr[...] * jnp.concatenate([(alpha * linv).T] * (D // LANES), axis=0)
        rows = lax.broadcasted_iota(jnp.int32, oT.shape, 0)
        cols = lax.broadcasted_iota(jnp.int32, oT.shape, 1)
        oT = jnp.where(rows // HEAD_DIM == cols // nq, oT, 0.0)
        shift = nq
        while shift < LANES:
            oT = oT + pltpu.roll(oT, shift, axis=1)
            shift *= 2
        o = oT.T[0:nq]

        o2 = _dot(p.astype(BF16), v_new) * _tile_lanes(linv, D // LANES)
        r2 = lax.broadcasted_iota(jnp.int32, o2.shape, 0)
        c2 = lax.broadcasted_iota(jnp.int32, o2.shape, 1)
        o2 = jnp.where(c2 // HEAD_DIM == r2 // nq, o2, 0.0)
        for h in range(N_HEADS):
            o = o + o2[h * nq:(h + 1) * nq]
        o_ref[0] = (o * gate_ref[0].astype(F32)).astype(BF16)

    return begin, chunk, finish


N_PROMPT_IN = 9


def _attn_kernel(pt_ref, *refs):
    del pt_ref
    n_sample_in = 3 * PAGES_PER_STEP + 6
    prompt_in = refs[:N_PROMPT_IN]
    sample_in = refs[N_PROMPT_IN:N_PROMPT_IN + n_sample_in]
    o_ref, os_ref, kbuf, vbuf, qblk, gcarry, m_scr, l_scr, acc_scr = refs[N_PROMPT_IN + n_sample_in:]
    hp = pl.program_id(1)
    j = pl.program_id(2)
    nj = kbuf.shape[0] // 2
    n_chunks = (D_MODEL // LANES) * nj // SEQS_PER_BATCH_STEP
    c = (hp * nj + j) % n_chunks
    begin, chunk, finish = _sample_attn_parts(sample_in + (os_ref, qblk, gcarry, m_scr, l_scr, acc_scr))
    pl.when(c == 0)(begin)
    _prompt_attn_step(hp, j, *prompt_in, o_ref, kbuf, vbuf, chunk)
    pl.when(c == n_chunks - 1)(finish)


def _attention(qT, qfT, kT, kfT, vT, gateT, page_table, ckT, cvT, clT, knew, vnew, lnewT, qs, gates, tri_u):
    B, D, S = qT.shape
    NB, NQ, _ = qs.shape
    T = ATT_T
    n = S // T
    HP, NJ = D // LANES, n // 2
    P = PAGES_PER_STEP
    PG = ckT.shape[2]
    n_chunks = page_table.shape[1] // P
    assert HP * NJ == SEQS_PER_BATCH_STEP * n_chunks and NB == B * SEQS_PER_BATCH_STEP

    lo = lambda rows: pl.BlockSpec((1, LANES, T), lambda b, hp, j, pt: (b, hp if rows else 0, j))
    hi = lambda rows: pl.BlockSpec((1, LANES, T), lambda b, hp, j, pt: (b, hp if rows else 0, n - 1 - j))
    kspec = pl.BlockSpec((1, LANES, S), lambda b, hp, j, pt: (b, hp, 0))
    prompt_specs = [lo(True), hi(True), lo(False), hi(False),
                    kspec, pl.BlockSpec((1, LANES, S), lambda b, hp, j, pt: (b, 0, 0)), kspec, lo(True), hi(True)]

    seq = lambda b, hp, j: b * SEQS_PER_BATCH_STEP + (hp * NJ + j) // n_chunks

    def page_spec(rows, jj):
        return pl.BlockSpec(
            (1, rows, PG),
            lambda b, hp, j, pt: (pt[seq(b, hp, j), ((hp * NJ + j) % n_chunks) * P + jj], 0, 0))

    per_seq = lambda rows, width: pl.BlockSpec((1, rows, width), lambda b, hp, j, pt: (seq(b, hp, j), 0, 0))
    sample_specs = ([page_spec(D, jj) for jj in range(P)] + [page_spec(D, jj) for jj in range(P)]
                    + [page_spec(N_HEADS, jj) for jj in range(P)]
                    + [per_seq(NQ, D), per_seq(NQ, D), per_seq(N_HEADS, PG), per_seq(NQ, D), per_seq(NQ, D),
                       pl.BlockSpec(tri_u.shape, lambda b, hp, j, pt: (0, 0))])
    assert len(prompt_specs) == N_PROMPT_IN
    grid_spec = pltpu.PrefetchScalarGridSpec(
        num_scalar_prefetch=1,
        grid=(B, HP, NJ),
        in_specs=prompt_specs + sample_specs,
        out_specs=[pl.BlockSpec((2, 1, LANES, T), lambda b, hp, j, pt: (0, b, hp, j)), per_seq(NQ, D)],
        scratch_shapes=[pltpu.VMEM((n, T, 2 * LANES), BF16), pltpu.VMEM((n, LANES, T), BF16),
                        pltpu.VMEM((LANES, D), BF16),
                        pltpu.VMEM((N_HEADS, LANES), F32),
                        pltpu.VMEM((LANES, LANES), F32), pltpu.VMEM((LANES, LANES), F32),
                        pltpu.VMEM((D, LANES), F32)],
    )
    return pl.pallas_call(
        _attn_kernel,
        grid_spec=grid_spec,
        out_shape=[jax.ShapeDtypeStruct((2, B, D, S // 2), BF16), jax.ShapeDtypeStruct((NB, NQ, D), BF16)],
        compiler_params=pltpu.CompilerParams(
            dimension_semantics=("arbitrary", "arbitrary", "arbitrary"), vmem_limit_bytes=VMEM_LIMIT),
        name="attn",
    )(page_table, qT, qT, qfT, qfT, kT, kfT, vT, gateT, gateT,
      *([ckT] * P), *([cvT] * P), *([clT] * P), knew, vnew, lnewT, qs, gates, tri_u)


def _outproj_fm_kernel(ogT_ref, h_ref, w_ref, y_ref):
    T = ATT_T
    og = ogT_ref[0, 0]
    swapped = pl.program_id(1) >= pl.num_programs(1) // 2
    contract = lambda a: lax.dot_general(a, w_ref[...], (((0,), (0,)), ((), ())), preferred_element_type=F32)
    y_ref[0, 0:T] = h_ref[0, 0:T] + contract(jnp.where(swapped, og[:, T:2 * T], og[:, 0:T]))
    y_ref[0, T:2 * T] = h_ref[0, T:2 * T] + contract(jnp.where(swapped, og[:, 0:T], og[:, T:2 * T]))


def _outproj_fm(og2, h, w):
    _, B, D, S2 = og2.shape
    T = 2 * ATT_T
    n = 2 * S2 // T
    og_map = lambda b, t: (t // (n // 2), b, 0, jnp.where(t < n // 2, t, n - 1 - t))
    return pl.pallas_call(
        _outproj_fm_kernel,
        grid=(B, n),
        in_specs=[pl.BlockSpec((1, 1, D, T), og_map), pl.BlockSpec((1, T, D), lambda b, t: (b, t, 0)),
                  pl.BlockSpec(w.shape, lambda b, t: (0, 0), pipeline_mode=pl.Buffered(1))],
        out_specs=pl.BlockSpec((1, T, D), lambda b, t: (b, t, 0)),
        out_shape=jax.ShapeDtypeStruct((B, 2 * S2, D), F32),
        compiler_params=pltpu.CompilerParams(
            dimension_semantics=("arbitrary", "arbitrary"), vmem_limit_bytes=VMEM_LIMIT),
        name="outproj_prompt",
    )(og2, h, w)


def _outproj_kernel(og_ref, h_ref, w_ref, y_ref):
    y_ref[...] = h_ref[...] + _dot(og_ref[...], w_ref[...])


def _outproj(og, h, w):
    R, D = h.shape
    T = min(OUT_T, R)
    row = pl.BlockSpec((T, D), lambda i: (i, 0))
    return pl.pallas_call(
        _outproj_kernel,
        grid=(R // T,),
        in_specs=[row, row, pl.BlockSpec(w.shape, lambda i: (0, 0), pipeline_mode=pl.Buffered(1))],
        out_specs=row,
        out_shape=jax.ShapeDtypeStruct((R, D), F32),
        compiler_params=pltpu.CompilerParams(
            dimension_semantics=("arbitrary",), vmem_limit_bytes=VMEM_LIMIT),
        name="outproj",
    )(og, h, w)


def _upper_tri(n):
    return (jnp.arange(n)[:, None] <= jnp.arange(n)[None, :]).astype(BF16)


def kernel(x_prompt, x_sample, state_conv, cache_k, cache_v, cache_logf, page_table,
           a_norm, a_w_in, a_conv, a_w_out, kv_norm, kv_w, kv_fbias, k_norm,
           b_norm, b_w_in, q_norm, b_w_out):
    B, S, D = x_prompt.shape
    NB, NQ, _ = x_sample.shape
    n_pool, PG = cache_k.shape[0], cache_k.shape[1]
    H, hd = N_HEADS, HEAD_DIM
    assert a_norm.shape[0] == 1 and b_norm.shape[0] == 1, "one conv layer then one attention layer"
    assert D == D_MODEL and NQ == SUBLANES and PG == LANES

    common = [a_norm[0][None], a_w_in[0].astype(BF16), a_conv[0], a_w_out[0].astype(BF16),
              kv_norm[None], b_norm[0][None]]
    b_w_in_b = b_w_in[0].astype(BF16)
    kv_wT = kv_w.T
    kv_wT_ext = jnp.concatenate([kv_wT[:2 * D], jnp.repeat(kv_wT[2 * D:], AUG, axis=0), kv_wT[2 * D:]], axis=0)
    fbias_col = jnp.broadcast_to(jnp.concatenate([jnp.repeat(kv_fbias, AUG), kv_fbias])[:, None],
                                 (LANES + H, LANES))
    k_norm_col = jnp.broadcast_to(jnp.tile(k_norm, H)[:, None], (D, LANES))
    q_norm_col = jnp.broadcast_to(jnp.tile(q_norm[0], H)[:, None], (D, LANES))
    kv_wT_b = kv_wT_ext.astype(BF16)
    w_prompt = common + [kv_wT_b, fbias_col, k_norm_col, _upper_tri(PRE_T), b_w_in_b.T, q_norm_col]
    w_sample = common + [kv_wT_b, kv_fbias[None], jnp.tile(k_norm, 2)[None], b_w_in_b, jnp.tile(q_norm[0], 2)[None]]
    w_out_b = b_w_out[0].astype(BF16)

    (h1, kT, vT, lfT, qT, qfT, kfT, gateT, tail) = _pre_prompt(x_prompt, w_prompt)
    st = state_conv[0]
    p1 = jnp.pad(st[:, 1:2], ((0, 0), (0, NQ - 1), (0, 0))).reshape(NB * NQ, D)
    p2 = jnp.pad(st, ((0, 0), (0, NQ - 2), (0, 0))).reshape(NB * NQ, D)
    (h1s, k_s, v_s, lf_s, qs, gate_s, cv_s) = _pre_sample(x_sample.reshape(NB * NQ, D), p1, p2, w_sample)

    lnewT = jnp.pad(lf_s.reshape(NB, NQ, H), ((0, 0), (0, PG - NQ), (0, 0))).transpose(0, 2, 1)
    og2, ogs = _attention(
        qT, qfT, kT, kfT, vT, gateT, page_table,
        cache_k.transpose(0, 2, 3, 1).reshape(n_pool, D, PG),
        cache_v.transpose(0, 2, 3, 1).reshape(n_pool, D, PG),
        cache_logf.transpose(0, 2, 1),
        k_s.reshape(NB, NQ, D), v_s.reshape(NB, NQ, D), lnewT,
        qs.reshape(NB, NQ, D), gate_s.reshape(NB, NQ, D), _upper_tri(PG))
    y_prompt = _outproj_fm(og2, h1, w_out_b)
    y_sample = _outproj(ogs.reshape(NB * NQ, D), h1s, w_out_b).reshape(NB, NQ, D)

    conv_prompt = tail[:, SUBLANES - (CONV_W - 1):][None]
    conv_sample = cv_s.reshape(NB, NQ, D)[:, NQ - (CONV_W - 1):][None]
    to_bshd = lambda t: t.reshape(B, H, hd, S).transpose(0, 3, 1, 2)
    return (y_prompt, y_sample, conv_prompt, conv_sample,
            to_bshd(kT), to_bshd(vT), lfT.transpose(0, 2, 1),
            k_s.reshape(NB, NQ, H, hd), v_s.reshape(NB, NQ, H, hd), lf_s.reshape(NB, NQ, H))
```

```python
import jax
import jax.numpy as jnp
from jax import lax
from jax.experimental import pallas as pl
from jax.experimental.pallas import tpu as pltpu

F32 = jnp.float32
BF16 = jnp.bfloat16

D_MODEL = 1024
N_HEADS = 16
HEAD_DIM = 64
CONV_W = 3
EPS = 1e-6
LOG2E = 1.4426950408889634
NEG_BIG = -1e30

LANES = 128
SUBLANES = 8
AUG = LANES // N_HEADS
VMEM_LIMIT = 56 * 1024 * 1024

PRE_T = 512
ATT_T = 256
OUT_T = 512
PAGES_PER_STEP = 8
SEQS_PER_BATCH_STEP = 4


def _dot(a, b):
    return jnp.dot(a, b, preferred_element_type=F32)


def _dot_nt(a, b):
    return lax.dot_general(a, b, (((1,), (1,)), ((), ())), preferred_element_type=F32)


def _tile_lanes(x, n):
    return jnp.concatenate([x] * n, axis=1) if n > 1 else x


def _split3(x):
    hi = x.astype(BF16)
    r1 = x - hi.astype(F32)
    mid = r1.astype(BF16)
    r2 = r1 - mid.astype(F32)
    return hi, mid, r2.astype(BF16)


def _dot_exact_rhs(m_bf16, x_f32):
    hi, mid, lo = _split3(x_f32)
    return _dot(m_bf16, hi) + _dot(m_bf16, mid) + _dot(m_bf16, lo)


def _dot_exact_lhs(x_f32, m_bf16):
    hi, mid, lo = _split3(x_f32)
    return _dot(hi, m_bf16) + _dot(mid, m_bf16) + _dot(lo, m_bf16)


def _rms(x, g):
    r = lax.rsqrt(jnp.mean(x * x, axis=-1, keepdims=True) + EPS)
    return x * r * g


def _head_rms(x, g128):
    lane = lax.broadcasted_iota(jnp.int32, (1, LANES), 1)
    lo_half = lane < HEAD_DIM
    outs = []
    for p in range(x.shape[1] // LANES):
        xp = x[:, p * LANES:(p + 1) * LANES]
        sq = xp * xp
        s_lo = jnp.sum(jnp.where(lo_half, sq, 0.0), axis=-1, keepdims=True)
        s_hi = jnp.sum(jnp.where(lo_half, 0.0, sq), axis=-1, keepdims=True)
        r = jnp.where(lo_half, lax.rsqrt(s_lo / HEAD_DIM + EPS), lax.rsqrt(s_hi / HEAD_DIM + EPS))
        outs.append(xp * r * g128)
    return jnp.concatenate(outs, axis=1)


def _head_rms_fm(xT, gT):
    n, T = xT.shape
    x3 = xT.reshape(n // HEAD_DIM, HEAD_DIM, T)
    r = lax.rsqrt(jnp.mean(x3 * x3, axis=1, keepdims=True) + EPS)
    return (x3 * r).reshape(n, T) * gT


def _silu(z):
    return z * jax.nn.sigmoid(z)


def _log_sigmoid(x):
    return jnp.minimum(x, 0.0) - jnp.log1p(jnp.exp(-jnp.abs(x)))


def _bias_select(F, j, key_side):
    hi, mid, lo = (t.astype(F32) for t in _split3(F))
    if key_side:
        out = jnp.where(j < 3, 1.0, jnp.where(j == 3, -hi, jnp.where(j == 4, -mid, jnp.where(j == 5, -lo, 0.0))))
    else:
        out = jnp.where(j == 0, hi, jnp.where(j == 1, mid, jnp.where(j == 2, lo, jnp.where(j < 6, 1.0, 0.0))))
    return out.astype(BF16)


def _conv_layer(x, prev1, prev2, m1, m2, w):
    D = D_MODEL
    xn = _rms(x, w['a_norm'][...]).astype(BF16)
    w_in = w['a_w_in']
    v = _dot(xn, w_in[:, D:2 * D]) * _dot(xn, w_in[:, 2 * D:3 * D])
    v1 = jnp.where(m1, prev1, pltpu.roll(v, 1, axis=0))
    v2 = jnp.where(m2, prev2, pltpu.roll(v, 2, axis=0))
    cw = w['a_conv'][...]
    conv = v2 * cw[0:1] + v1 * cw[1:2] + v * cw[2:3]
    b = _dot(xn, w_in[:, 0:D])
    z = _dot(xn, w_in[:, 3 * D:4 * D])
    g = (_silu(z) * b * conv).astype(BF16)
    return v, x + _dot(g, w['a_w_out'][...])


def _fox_proj(h1, w):
    D = D_MODEL
    xn = _rms(h1, w['b_norm'][...]).astype(BF16)
    b_w_in = w['b_w_in']
    q = _head_rms(_dot(xn, b_w_in[:, 0:D]), w['q_norm'][...]) * (LOG2E * HEAD_DIM ** -0.5)
    gate = _silu(_dot(xn, b_w_in[:, D:2 * D]))
    return q.astype(BF16), gate.astype(BF16)


_W_COMMON = ('a_norm', 'a_w_in', 'a_conv', 'a_w_out', 'kv_norm', 'b_norm')
_W_PROMPT = _W_COMMON + ('kv_wT', 'fbias_col', 'k_norm_col', 'tri_u', 'b_w_inT', 'q_norm_col')
_W_SAMPLE = _W_COMMON + ('kv_wT', 'fbias_row', 'k_norm_row', 'b_w_in', 'q_norm')


def _pre_prompt_kernel(x_ref, *refs):
    nw = len(_W_PROMPT)
    w = dict(zip(_W_PROMPT, refs[:nw]))
    (h1_ref, kT_ref, vT_ref, lfT_ref, qT_ref, qfT_ref, kfT_ref, gateT_ref, tail_ref, vcarry, fcarry) = refs[nw:]
    T = x_ref.shape[1]
    D = D_MODEL
    reps = T // LANES

    @pl.when(pl.program_id(1) == 0)
    def _():
        vcarry[...] = jnp.zeros_like(vcarry)
        fcarry[...] = jnp.zeros_like(fcarry)

    row = lax.broadcasted_iota(jnp.int32, (T, 1), 0)
    prev = vcarry[...]
    pm1 = prev[SUBLANES - 1:SUBLANES]
    pm2 = prev[SUBLANES - 2:SUBLANES - 1]
    v, h1 = _conv_layer(x_ref[0], pm1, jnp.where(row == 0, pm2, pm1), row == 0, row < 2, w)
    tail = v[T - SUBLANES:T]
    vcarry[...] = tail
    tail_ref[0] = tail
    h1_ref[0] = h1

    hn = _rms(h1, w['kv_norm'][...]).astype(BF16)
    kvT = w['kv_wT']
    kT_ref[0] = _head_rms_fm(_dot_nt(kvT[0:D, :], hn), _tile_lanes(w['k_norm_col'][...], reps))
    vT_ref[0] = _dot_nt(kvT[D:2 * D, :], hn)
    fT = _dot_nt(kvT[2 * D:2 * D + LANES + N_HEADS, :], hn) + _tile_lanes(w['fbias_col'][...], reps)
    lfT = _log_sigmoid(fT)
    lfT_ref[0] = lfT[LANES:LANES + N_HEADS]
    FT = _dot_exact_lhs(lfT[0:LANES], w['tri_u'][...]) + _tile_lanes(fcarry[...], reps)
    fcarry[...] = jnp.broadcast_to(FT[:, T - 1:T], fcarry.shape)
    Fp = FT * LOG2E
    slot = lax.broadcasted_iota(jnp.int32, (LANES, 1), 0) % AUG
    kfT_ref[0] = _bias_select(Fp, slot, True)
    qfT_ref[0] = _bias_select(Fp, slot, False)

    xn2 = _rms(h1, w['b_norm'][...]).astype(BF16)
    b_w_inT = w['b_w_inT']
    qT = _head_rms_fm(_dot_nt(b_w_inT[0:D, :], xn2), _tile_lanes(w['q_norm_col'][...], reps))
    qT_ref[0] = (qT * (LOG2E * HEAD_DIM ** -0.5)).astype(BF16)
    gateT_ref[0] = _silu(_dot_nt(b_w_inT[D:2 * D, :], xn2)).astype(BF16)


def _pre_sample_kernel(x_ref, p1_ref, p2_ref, *refs):
    nw = len(_W_SAMPLE)
    w = dict(zip(_W_SAMPLE, refs[:nw]))
    h1_ref, k_ref, v_ref, lf_ref, q_ref, gate_ref, cv_ref = refs[nw:]
    D = D_MODEL
    t = lax.broadcasted_iota(jnp.int32, (x_ref.shape[0], 1), 0) % SUBLANES
    v, h1 = _conv_layer(x_ref[...], p1_ref[...], p2_ref[...], t == 0, t < 2, w)
    cv_ref[...] = v
    h1_ref[...] = h1
    hn = _rms(h1, w['kv_norm'][...]).astype(BF16)
    kvT = w['kv_wT']
    k_ref[...] = _head_rms(_dot_nt(hn, kvT[0:D, :]), w['k_norm_row'][...])
    v_ref[...] = _dot_nt(hn, kvT[D:2 * D, :])
    lf_ref[...] = _log_sigmoid(_dot_nt(hn, kvT[2 * D + LANES:2 * D + LANES + N_HEADS, :]) + w['fbias_row'][...])
    q, gate = _fox_proj(h1, w)
    q_ref[...] = q
    gate_ref[...] = gate


def _pre_prompt(x, weights):
    B, S, D = x.shape
    T = PRE_T
    tok = lambda d: pl.BlockSpec((1, T, d), lambda b, j: (b, j, 0))
    feat = lambda n: pl.BlockSpec((1, n, T), lambda b, j: (b, 0, j))
    out_shape = [
        jax.ShapeDtypeStruct((B, S, D), F32),
        jax.ShapeDtypeStruct((B, D, S), F32),
        jax.ShapeDtypeStruct((B, D, S), F32),
        jax.ShapeDtypeStruct((B, N_HEADS, S), F32),
        jax.ShapeDtypeStruct((B, D, S), BF16),
        jax.ShapeDtypeStruct((B, LANES, S), BF16),
        jax.ShapeDtypeStruct((B, LANES, S), BF16),
        jax.ShapeDtypeStruct((B, D, S), BF16),
        jax.ShapeDtypeStruct((B, SUBLANES, D), F32),
    ]
    out_specs = [tok(D), feat(D), feat(D), feat(N_HEADS), feat(D), feat(LANES), feat(LANES), feat(D),
                 pl.BlockSpec((1, SUBLANES, D), lambda b, j: (b, 0, 0))]
    const = lambda b, j: (0, 0)
    w_specs = [pl.BlockSpec(w.shape, const, pipeline_mode=pl.Buffered(1)) for w in weights]
    return pl.pallas_call(
        _pre_prompt_kernel,
        grid=(B, S // T),
        in_specs=[tok(D)] + w_specs,
        out_specs=out_specs,
        out_shape=out_shape,
        scratch_shapes=[pltpu.VMEM((SUBLANES, D), F32), pltpu.VMEM((LANES, LANES), F32)],
        compiler_params=pltpu.CompilerParams(
            dimension_semantics=("arbitrary", "arbitrary"), vmem_limit_bytes=VMEM_LIMIT),
        name="pre_prompt",
    )(x, *weights)


def _pre_sample(x2, p1, p2, weights):
    R, D = x2.shape
    out_shape = [
        jax.ShapeDtypeStruct((R, D), F32),
        jax.ShapeDtypeStruct((R, D), F32),
        jax.ShapeDtypeStruct((R, D), F32),
        jax.ShapeDtypeStruct((R, N_HEADS), F32),
        jax.ShapeDtypeStruct((R, D), BF16),
        jax.ShapeDtypeStruct((R, D), BF16),
        jax.ShapeDtypeStruct((R, D), F32),
    ]
    return pl.pallas_call(
        _pre_sample_kernel,
        out_shape=out_shape,
        compiler_params=pltpu.CompilerParams(vmem_limit_bytes=VMEM_LIMIT),
        name="pre_sample",
    )(x2, p1, p2, *weights)


def _prompt_attn_step(hp, j, qa_ref, qb_ref, qfa_ref, qfb_ref, kT_ref, kfT_ref, vT_ref, ga_ref, gb_ref,
                      o_ref, kbuf, vbuf, alongside):
    T = qa_ref.shape[2]
    n = kbuf.shape[0]

    @pl.when(j == 0)
    def _():
        for kt in range(n):
            cols = slice(kt * T, (kt + 1) * T)
            kbuf[kt, :, 0:LANES] = kT_ref[0, :, cols].T.astype(BF16)
            kbuf[kt, :, LANES:2 * LANES] = kfT_ref[0, :, cols].astype(F32).T.astype(BF16)
            vbuf[kt] = vT_ref[0, :, cols].astype(BF16)

    row = lax.broadcasted_iota(jnp.int32, (LANES, 1), 0)
    lo_half = row < HEAD_DIM

    def query_operand(q_ref, qf_ref):
        q2 = q_ref[0]
        qf = qf_ref[0]
        zero = jnp.zeros_like(q2)
        return jnp.concatenate([
            jnp.concatenate([jnp.where(lo_half, q2, zero), jnp.where(row // AUG == 2 * hp, qf, zero)], axis=0),
            jnp.concatenate([jnp.where(lo_half, zero, q2), jnp.where(row // AUG == 2 * hp + 1, qf, zero)], axis=0),
        ], axis=1)

    def scores(q_ref, qf_ref, n_tiles):
        rhs = query_operand(q_ref, qf_ref)
        ss = [_dot(kbuf[kt], rhs) for kt in range(n_tiles)]
        r = lax.broadcasted_iota(jnp.int32, ss[-1].shape, 0)
        c = lax.broadcasted_iota(jnp.int32, ss[-1].shape, 1)
        ss[-1] = jnp.where(r <= jnp.where(c >= T, c - T, c), ss[-1], NEG_BIG)
        m = None
        for s in ss:
            ms = jnp.max(s, axis=0, keepdims=True)
            m = ms if m is None else jnp.maximum(m, ms)
        return ss, m

    def weigh(ss, m, g_ref, half):
        acc = l = None
        for kt, s in enumerate(ss):
            p = jnp.exp2(s - m)
            ps = jnp.sum(p, axis=0, keepdims=True)
            l = ps if l is None else l + ps
            d = _dot(vbuf[kt], p.astype(BF16))
            acc = d if acc is None else acc + d
        o = acc / l
        o = jnp.where(lo_half, o[:, 0:T], o[:, T:2 * T])
        o_ref[half, 0] = (o * g_ref[0].astype(F32)).astype(BF16)

    for v in range(n // 2):
        @pl.when(j == v)
        def _(v=v):
            alongside()
            sa = scores(qa_ref, qfa_ref, v + 1)
            sb = scores(qb_ref, qfb_ref, n - v)
            weigh(*sa, ga_ref, 0)
            weigh(*sb, gb_ref, 1)


def _sample_attn_parts(refs):
    P = PAGES_PER_STEP
    sub = P // 2
    kpages = refs[0:P]
    vpages = refs[P:2 * P]
    lpages = refs[2 * P:3 * P]
    (knew_ref, vnew_ref, lnew_ref, q_ref, gate_ref, tri_ref,
     o_ref, qblk, gcarry, m_scr, l_scr, acc_scr) = refs[3 * P:]
    nq = q_ref.shape[1]
    R = LANES
    PG = LANES
    D = D_MODEL

    def begin():
        q = q_ref[0]
        rows = lax.broadcasted_iota(jnp.int32, (R, D), 0)
        cols = lax.broadcasted_iota(jnp.int32, (R, D), 1)
        tiled = jnp.concatenate([q] * (R // nq), axis=0)
        qblk[...] = jnp.where(cols // HEAD_DIM == rows // nq, tiled, jnp.zeros_like(tiled))
        gcarry[...] = jnp.zeros_like(gcarry)
        m_scr[...] = jnp.full_like(m_scr, NEG_BIG)
        l_scr[...] = jnp.zeros_like(l_scr)
        acc_scr[...] = jnp.zeros_like(acc_scr)

    def bias_rows(lf_pages):
        cs = _dot_exact_lhs(jnp.concatenate(lf_pages, axis=0), tri_ref[...])
        carry = gcarry[...]
        out = []
        for j in range(len(lf_pages)):
            G = cs[j * N_HEADS:(j + 1) * N_HEADS] + carry
            carry = jnp.broadcast_to(G[:, PG - 1:PG], carry.shape)
            out.append(jnp.concatenate(
                [jnp.broadcast_to(G[h:h + 1], (nq, PG)) for h in range(N_HEADS)], axis=0))
        gcarry[...] = carry
        return jnp.concatenate(out, axis=1) if len(out) > 1 else out[0]

    def update(s, vT):
        m_prev = m_scr[...]
        m_new = jnp.maximum(m_prev, jnp.max(s, axis=1, keepdims=True))
        alpha = jnp.exp2(m_prev - m_new)
        p = jnp.exp2(s - _tile_lanes(m_new, s.shape[1] // LANES))
        l_scr[...] = alpha * l_scr[...] + jnp.sum(p, axis=1, keepdims=True)
        acc_scr[...] = jnp.concatenate([alpha.T] * (D // LANES), axis=0) * acc_scr[...] + _dot_nt(vT, p.astype(BF16))
        m_scr[...] = m_new

    def chunk():
        bias = bias_rows([lpages[j][0] for j in range(P)]) * LOG2E
        subs = []
        for h0 in range(0, P, sub):
            kT = jnp.concatenate([kpages[j][0].astype(BF16) for j in range(h0, h0 + sub)], axis=1)
            subs.append(_dot(qblk[...], kT) - bias[:, h0 * PG:(h0 + sub) * PG])
        for h0, s in zip(range(0, P, sub), subs):
            update(s, jnp.concatenate([vpages[j][0].astype(BF16) for j in range(h0, h0 + sub)], axis=1))

    def finish():
        pad = jnp.zeros((PG - nq, D), F32)
        k_new = jnp.concatenate([knew_ref[0], pad], axis=0).astype(BF16)
        v_new = jnp.concatenate([vnew_ref[0], pad], axis=0).astype(BF16)
        s = _dot_nt(qblk[...], k_new) - bias_rows([lnew_ref[0]]) * LOG2E
        key = lax.broadcasted_iota(jnp.int32, s.shape, 1)
        qi = lax.broadcasted_iota(jnp.int32, s.shape, 0) % nq
        s = jnp.where(key <= qi, s, NEG_BIG)
        m_prev = m_scr[...]
        m_new = jnp.maximum(m_prev, jnp.max(s, axis=1, keepdims=True))
        alpha = jnp.exp2(m_prev - m_new)
        p = jnp.exp2(s - m_new)
        linv = 1.0 / (alpha * l_scr[...] + jnp.sum(p, axis=1, keepdims=True))

        oT = acc_scr[...] * jnp.concatenate([(alpha * linv).T] * (D // LANES), axis=0)
        rows = lax.broadcasted_iota(jnp.int32, oT.shape, 0)
        cols = lax.broadcasted_iota(jnp.int32, oT.shape, 1)
        oT = jnp.where(rows // HEAD_DIM == cols // nq, oT, 0.0)
        shift = nq
        while shift < LANES:
            oT = oT + pltpu.roll(oT, shift, axis=1)
            shift *= 2
        o = oT.T[0:nq]

        o2 = _dot(p.astype(BF16), v_new) * _tile_lanes(linv, D // LANES)
        r2 = lax.broadcasted_iota(jnp.int32, o2.shape, 0)
        c2 = lax.broadcasted_iota(jnp.int32, o2.shape, 1)
        o2 = jnp.where(c2 // HEAD_DIM == r2 // nq, o2, 0.0)
        for h in range(N_HEADS):
            o = o + o2[h * nq:(h + 1) * nq]
        o_ref[0] = (o * gate_ref[0].astype(F32)).astype(BF16)

    return begin, chunk, finish


N_PROMPT_IN = 9


def _attn_kernel(pt_ref, *refs):
    del pt_ref
    n_sample_in = 3 * PAGES_PER_STEP + 6
    prompt_in = refs[:N_PROMPT_IN]
    sample_in = refs[N_PROMPT_IN:N_PROMPT_IN + n_sample_in]
    o_ref, os_ref, kbuf, vbuf, qblk, gcarry, m_scr, l_scr, acc_scr = refs[N_PROMPT_IN + n_sample_in:]
    hp = pl.program_id(1)
    j = pl.program_id(2)
    nj = kbuf.shape[0] // 2
    n_chunks = (D_MODEL // LANES) * nj // SEQS_PER_BATCH_STEP
    c = (hp * nj + j) % n_chunks
    begin, chunk, finish = _sample_attn_parts(sample_in + (os_ref, qblk, gcarry, m_scr, l_scr, acc_scr))
    pl.when(c == 0)(begin)
    _prompt_attn_step(hp, j, *prompt_in, o_ref, kbuf, vbuf, chunk)
    pl.when(c == n_chunks - 1)(finish)


def _attention(qT, qfT, kT, kfT, vT, gateT, page_table, ckT, cvT, clT, knew, vnew, lnewT, qs, gates, tri_u):
    B, D, S = qT.shape
    NB, NQ, _ = qs.shape
    T = ATT_T
    n = S // T
    HP, NJ = D // LANES, n // 2
    P = PAGES_PER_STEP
    PG = ckT.shape[2]
    n_chunks = page_table.shape[1] // P
    assert HP * NJ == SEQS_PER_BATCH_STEP * n_chunks and NB == B * SEQS_PER_BATCH_STEP

    lo = lambda rows: pl.BlockSpec((1, LANES, T), lambda b, hp, j, pt: (b, hp if rows else 0, j))
    hi = lambda rows: pl.BlockSpec((1, LANES, T), lambda b, hp, j, pt: (b, hp if rows else 0, n - 1 - j))
    kspec = pl.BlockSpec((1, LANES, S), lambda b, hp, j, pt: (b, hp, 0))
    prompt_specs = [lo(True), hi(True), lo(False), hi(False),
                    kspec, pl.BlockSpec((1, LANES, S), lambda b, hp, j, pt: (b, 0, 0)), kspec, lo(True), hi(True)]

    seq = lambda b, hp, j: b * SEQS_PER_BATCH_STEP + (hp * NJ + j) // n_chunks

    def page_spec(rows, jj):
        return pl.BlockSpec(
            (1, rows, PG),
            lambda b, hp, j, pt: (pt[seq(b, hp, j), ((hp * NJ + j) % n_chunks) * P + jj], 0, 0))

    per_seq = lambda rows, width: pl.BlockSpec((1, rows, width), lambda b, hp, j, pt: (seq(b, hp, j), 0, 0))
    sample_specs = ([page_spec(D, jj) for jj in range(P)] + [page_spec(D, jj) for jj in range(P)]
                    + [page_spec(N_HEADS, jj) for jj in range(P)]
                    + [per_seq(NQ, D), per_seq(NQ, D), per_seq(N_HEADS, PG), per_seq(NQ, D), per_seq(NQ, D),
                       pl.BlockSpec(tri_u.shape, lambda b, hp, j, pt: (0, 0))])
    assert len(prompt_specs) == N_PROMPT_IN
    grid_spec = pltpu.PrefetchScalarGridSpec(
        num_scalar_prefetch=1,
        grid=(B, HP, NJ),
        in_specs=prompt_specs + sample_specs,
        out_specs=[pl.BlockSpec((2, 1, LANES, T), lambda b, hp, j, pt: (0, b, hp, j)), per_seq(NQ, D)],
        scratch_shapes=[pltpu.VMEM((n, T, 2 * LANES), BF16), pltpu.VMEM((n, LANES, T), BF16),
                        pltpu.VMEM((LANES, D), BF16),
                        pltpu.VMEM((N_HEADS, LANES), F32),
                        pltpu.VMEM((LANES, LANES), F32), pltpu.VMEM((LANES, LANES), F32),
                        pltpu.VMEM((D, LANES), F32)],
    )
    return pl.pallas_call(
        _attn_kernel,
        grid_spec=grid_spec,
        out_shape=[jax.ShapeDtypeStruct((2, B, D, S // 2), BF16), jax.ShapeDtypeStruct((NB, NQ, D), BF16)],
        compiler_params=pltpu.CompilerParams(
            dimension_semantics=("arbitrary", "arbitrary", "arbitrary"), vmem_limit_bytes=VMEM_LIMIT),
        name="attn",
    )(page_table, qT, qT, qfT, qfT, kT, kfT, vT, gateT, gateT,
      *([ckT] * P), *([cvT] * P), *([clT] * P), knew, vnew, lnewT, qs, gates, tri_u)


def _outproj_fm_kernel(ogT_ref, h_ref, w_ref, y_ref):
    T = ATT_T
    og = ogT_ref[0, 0]
    swapped = pl.program_id(1) >= pl.num_programs(1) // 2
    contract = lambda a: lax.dot_general(a, w_ref[...], (((0,), (0,)), ((), ())), preferred_element_type=F32)
    y_ref[0, 0:T] = h_ref[0, 0:T] + contract(jnp.where(swapped, og[:, T:2 * T], og[:, 0:T]))
    y_ref[0, T:2 * T] = h_ref[0, T:2 * T] + contract(jnp.where(swapped, og[:, 0:T], og[:, T:2 * T]))


def _outproj_fm(og2, h, w):
    _, B, D, S2 = og2.shape
    T = 2 * ATT_T
    n = 2 * S2 // T
    og_map = lambda b, t: (t // (n // 2), b, 0, jnp.where(t < n // 2, t, n - 1 - t))
    return pl.pallas_call(
        _outproj_fm_kernel,
        grid=(B, n),
        in_specs=[pl.BlockSpec((1, 1, D, T), og_map), pl.BlockSpec((1, T, D), lambda b, t: (b, t, 0)),
                  pl.BlockSpec(w.shape, lambda b, t: (0, 0), pipeline_mode=pl.Buffered(1))],
        out_specs=pl.BlockSpec((1, T, D), lambda b, t: (b, t, 0)),
        out_shape=jax.ShapeDtypeStruct((B, 2 * S2, D), F32),
        compiler_params=pltpu.CompilerParams(
            dimension_semantics=("arbitrary", "arbitrary"), vmem_limit_bytes=VMEM_LIMIT),
        name="outproj_prompt",
    )(og2, h, w)


def _outproj_kernel(og_ref, h_ref, w_ref, y_ref):
    y_ref[...] = h_ref[...] + _dot(og_ref[...], w_ref[...])


def _outproj(og, h, w):
    R, D = h.shape
    T = min(OUT_T, R)
    row = pl.BlockSpec((T, D), lambda i: (i, 0))
    return pl.pallas_call(
        _outproj_kernel,
        grid=(R // T,),
        in_specs=[row, row, pl.BlockSpec(w.shape, lambda i: (0, 0), pipeline_mode=pl.Buffered(1))],
        out_specs=row,
        out_shape=jax.ShapeDtypeStruct((R, D), F32),
        compiler_params=pltpu.CompilerParams(
            dimension_semantics=("arbitrary",), vmem_limit_bytes=VMEM_LIMIT),
        name="outproj",
    )(og, h, w)


def _upper_tri(n):
    return (jnp.arange(n)[:, None] <= jnp.arange(n)[None, :]).astype(BF16)


def kernel(x_prompt, x_sample, state_conv, cache_k, cache_v, cache_logf, page_table,
           a_norm, a_w_in, a_conv, a_w_out, kv_norm, kv_w, kv_fbias, k_norm,
           b_norm, b_w_in, q_norm, b_w_out):
    B, S, D = x_prompt.shape
    NB, NQ, _ = x_sample.shape
    n_pool, PG = cache_k.shape[0], cache_k.shape[1]
    H, hd = N_HEADS, HEAD_DIM
    assert a_norm.shape[0] == 1 and b_norm.shape[0] == 1, "one conv layer then one attention layer"
    assert D == D_MODEL and NQ == SUBLANES and PG == LANES

    common = [a_norm[0][None], a_w_in[0].astype(BF16), a_conv[0], a_w_out[0].astype(BF16),
              kv_norm[None], b_norm[0][None]]
    b_w_in_b = b_w_in[0].astype(BF16)
    kv_wT = kv_w.T
    kv_wT_ext = jnp.concatenate([kv_wT[:2 * D], jnp.repeat(kv_wT[2 * D:], AUG, axis=0), kv_wT[2 * D:]], axis=0)
    fbias_col = jnp.broadcast_to(jnp.concatenate([jnp.repeat(kv_fbias, AUG), kv_fbias])[:, None],
                                 (LANES + H, LANES))
    k_norm_col = jnp.broadcast_to(jnp.tile(k_norm, H)[:, None], (D, LANES))
    q_norm_col = jnp.broadcast_to(jnp.tile(q_norm[0], H)[:, None], (D, LANES))
    kv_wT_b = kv_wT_ext.astype(BF16)
    w_prompt = common + [kv_wT_b, fbias_col, k_norm_col, _upper_tri(PRE_T), b_w_in_b.T, q_norm_col]
    w_sample = common + [kv_wT_b, kv_fbias[None], jnp.tile(k_norm, 2)[None], b_w_in_b, jnp.tile(q_norm[0], 2)[None]]
    w_out_b = b_w_out[0].astype(BF16)

    (h1, kT, vT, lfT, qT, qfT, kfT, gateT, tail) = _pre_prompt(x_prompt, w_prompt)
    st = state_conv[0]
    p1 = jnp.pad(st[:, 1:2], ((0, 0), (0, NQ - 1), (0, 0))).reshape(NB * NQ, D)
    p2 = jnp.pad(st, ((0, 0), (0, NQ - 2), (0, 0))).reshape(NB * NQ, D)
    (h1s, k_s, v_s, lf_s, qs, gate_s, cv_s) = _pre_sample(x_sample.reshape(NB * NQ, D), p1, p2, w_sample)

    lnewT = jnp.pad(lf_s.reshape(NB, NQ, H), ((0, 0), (0, PG - NQ), (0, 0))).transpose(0, 2, 1)
    og2, ogs = _attention(
        qT, qfT, kT, kfT, vT, gateT, page_table,
        cache_k.transpose(0, 2, 3, 1).reshape(n_pool, D, PG),
        cache_v.transpose(0, 2, 3, 1).reshape(n_pool, D, PG),
        cache_logf.transpose(0, 2, 1),
        k_s.reshape(NB, NQ, D), v_s.reshape(NB, NQ, D), lnewT,
        qs.reshape(NB, NQ, D), gate_s.reshape(NB, NQ, D), _upper_tri(PG))
    y_prompt = _outproj_fm(og2, h1, w_out_b)
    y_sample = _outproj(ogs.reshape(NB * NQ, D), h1s, w_out_b).reshape(NB, NQ, D)

    conv_prompt = tail[:, SUBLANES - (CONV_W - 1):][None]
    conv_sample = cv_s.reshape(NB, NQ, D)[:, NQ - (CONV_W - 1):][None]
    to_bshd = lambda t: t.reshape(B, H, hd, S).transpose(0, 3, 1, 2)
    return (y_prompt, y_sample, conv_prompt, conv_sample,
            to_bshd(kT), to_bshd(vT), lfT.transpose(0, 2, 1),
            k_s.reshape(NB, NQ, H, hd), v_s.reshape(NB, NQ, H, hd), lf_s.reshape(NB, NQ, H))
```

```python
import jax
import jax.numpy as jnp
from jax import lax
from jax.experimental import pallas as pl
from jax.experimental.pallas import tpu as pltpu

F32 = jnp.float32
BF16 = jnp.bfloat16

D_MODEL = 1024
N_HEADS = 16
HEAD_DIM = 64
CONV_W = 3
EPS = 1e-6
LOG2E = 1.4426950408889634
NEG_BIG = -1e30

LANES = 128
SUBLANES = 8
AUG = LANES // N_HEADS
VMEM_LIMIT = 56 * 1024 * 1024

PRE_T = 512
ATT_T = 256
OUT_T = 512
PAGES_PER_STEP = 8
SEQS_PER_BATCH_STEP = 4


def _dot(a, b):
    return jnp.dot(a, b, preferred_element_type=F32)


def _dot_nt(a, b):
    return lax.dot_general(a, b, (((1,), (1,)), ((), ())), preferred_element_type=F32)


def _tile_lanes(x, n):
    return jnp.concatenate([x] * n, axis=1) if n > 1 else x


def _split3(x):
    hi = x.astype(BF16)
    r1 = x - hi.astype(F32)
    mid = r1.astype(BF16)
    r2 = r1 - mid.astype(F32)
    return hi, mid, r2.astype(BF16)


def _dot_exact_rhs(m_bf16, x_f32):
    hi, mid, lo = _split3(x_f32)
    return _dot(m_bf16, hi) + _dot(m_bf16, mid) + _dot(m_bf16, lo)


def _dot_exact_lhs(x_f32, m_bf16):
    hi, mid, lo = _split3(x_f32)
    return _dot(hi, m_bf16) + _dot(mid, m_bf16) + _dot(lo, m_bf16)


def _rms(x, g):
    r = lax.rsqrt(jnp.mean(x * x, axis=-1, keepdims=True) + EPS)
    return x * r * g


def _head_rms(x, g128):
    lane = lax.broadcasted_iota(jnp.int32, (1, LANES), 1)
    lo_half = lane < HEAD_DIM
    outs = []
    for p in range(x.shape[1] // LANES):
        xp = x[:, p * LANES:(p + 1) * LANES]
        sq = xp * xp
        s_lo = jnp.sum(jnp.where(lo_half, sq, 0.0), axis=-1, keepdims=True)
        s_hi = jnp.sum(jnp.where(lo_half, 0.0, sq), axis=-1, keepdims=True)
        r = jnp.where(lo_half, lax.rsqrt(s_lo / HEAD_DIM + EPS), lax.rsqrt(s_hi / HEAD_DIM + EPS))
        outs.append(xp * r * g128)
    return jnp.concatenate(outs, axis=1)


def _head_rms_fm(xT, gT):
    n, T = xT.shape
    x3 = xT.reshape(n // HEAD_DIM, HEAD_DIM, T)
    r = lax.rsqrt(jnp.mean(x3 * x3, axis=1, keepdims=True) + EPS)
    return (x3 * r).reshape(n, T) * gT


def _silu(z):
    return z * jax.nn.sigmoid(z)


def _log_sigmoid(x):
    return jnp.minimum(x, 0.0) - jnp.log1p(jnp.exp(-jnp.abs(x)))


def _bias_select(F, j, key_side):
    hi, mid, lo = (t.astype(F32) for t in _split3(F))
    if key_side:
        out = jnp.where(j < 3, 1.0, jnp.where(j == 3, -hi, jnp.where(j == 4, -mid, jnp.where(j == 5, -lo, 0.0))))
    else:
        out = jnp.where(j == 0, hi, jnp.where(j == 1, mid, jnp.where(j == 2, lo, jnp.where(j < 6, 1.0, 0.0))))
    return out.astype(BF16)


def _conv_layer(x, prev1, prev2, m1, m2, w):
    D = D_MODEL
    xn = _rms(x, w['a_norm'][...]).astype(BF16)
    w_in = w['a_w_in']
    v = _dot(xn, w_in[:, D:2 * D]) * _dot(xn, w_in[:, 2 * D:3 * D])
    v1 = jnp.where(m1, prev1, pltpu.roll(v, 1, axis=0))
    v2 = jnp.where(m2, prev2, pltpu.roll(v, 2, axis=0))
    cw = w['a_conv'][...]
    conv = v2 * cw[0:1] + v1 * cw[1:2] + v * cw[2:3]
    b = _dot(xn, w_in[:, 0:D])
    z = _dot(xn, w_in[:, 3 * D:4 * D])
    g = (_silu(z) * b * conv).astype(BF16)
    return v, x + _dot(g, w['a_w_out'][...])


def _fox_proj(h1, w):
    D = D_MODEL
    xn = _rms(h1, w['b_norm'][...]).astype(BF16)
    b_w_in = w['b_w_in']
    q = _head_rms(_dot(xn, b_w_in[:, 0:D]), w['q_norm'][...]) * (LOG2E * HEAD_DIM ** -0.5)
    gate = _silu(_dot(xn, b_w_in[:, D:2 * D]))
    return q.astype(BF16), gate.astype(BF16)


_W_COMMON = ('a_norm', 'a_w_in', 'a_conv', 'a_w_out', 'kv_norm', 'b_norm')
_W_PROMPT = _W_COMMON + ('kv_wT', 'fbias_col', 'k_norm_col', 'tri_u', 'b_w_inT', 'q_norm_col')
_W_SAMPLE = _W_COMMON + ('kv_wT', 'fbias_row', 'k_norm_row', 'b_w_in', 'q_norm')


def _pre_prompt_kernel(x_ref, *refs):
    nw = len(_W_PROMPT)
    w = dict(zip(_W_PROMPT, refs[:nw]))
    (h1_ref, kT_ref, vT_ref, lfT_ref, qT_ref, qfT_ref, kfT_ref, gateT_ref, tail_ref, vcarry, fcarry) = refs[nw:]
    T = x_ref.shape[1]
    D = D_MODEL
    reps = T // LANES

    @pl.when(pl.program_id(1) == 0)
    def _():
        vcarry[...] = jnp.zeros_like(vcarry)
        fcarry[...] = jnp.zeros_like(fcarry)

    row = lax.broadcasted_iota(jnp.int32, (T, 1), 0)
    prev = vcarry[...]
    pm1 = prev[SUBLANES - 1:SUBLANES]
    pm2 = prev[SUBLANES - 2:SUBLANES - 1]
    v, h1 = _conv_layer(x_ref[0], pm1, jnp.where(row == 0, pm2, pm1), row == 0, row < 2, w)
    tail = v[T - SUBLANES:T]
    vcarry[...] = tail
    tail_ref[0] = tail
    h1_ref[0] = h1

    hn = _rms(h1, w['kv_norm'][...]).astype(BF16)
    kvT = w['kv_wT']
    kT_ref[0] = _head_rms_fm(_dot_nt(kvT[0:D, :], hn), _tile_lanes(w['k_norm_col'][...], reps))
    vT_ref[0] = _dot_nt(kvT[D:2 * D, :], hn)
    fT = _dot_nt(kvT[2 * D:2 * D + LANES + N_HEADS, :], hn) + _tile_lanes(w['fbias_col'][...], reps)
    lfT = _log_sigmoid(fT)
    lfT_ref[0] = lfT[LANES:LANES + N_HEADS]
    FT = _dot_exact_lhs(lfT[0:LANES], w['tri_u'][...]) + _tile_lanes(fcarry[...], reps)
    fcarry[...] = jnp.broadcast_to(FT[:, T - 1:T], fcarry.shape)
    Fp = FT * LOG2E
    slot = lax.broadcasted_iota(jnp.int32, (LANES, 1), 0) % AUG
    kfT_ref[0] = _bias_select(Fp, slot, True)
    qfT_ref[0] = _bias_select(Fp, slot, False)

    xn2 = _rms(h1, w['b_norm'][...]).astype(BF16)
    b_w_inT = w['b_w_inT']
    qT = _head_rms_fm(_dot_nt(b_w_inT[0:D, :], xn2), _tile_lanes(w['q_norm_col'][...], reps))
    qT_ref[0] = (qT * (LOG2E * HEAD_DIM ** -0.5)).astype(BF16)
    gateT_ref[0] = _silu(_dot_nt(b_w_inT[D:2 * D, :], xn2)).astype(BF16)


def _pre_sample_kernel(x_ref, p1_ref, p2_ref, *refs):
    nw = len(_W_SAMPLE)
    w = dict(zip(_W_SAMPLE, refs[:nw]))
    h1_ref, k_ref, v_ref, lf_ref, q_ref, gate_ref, cv_ref = refs[nw:]
    D = D_MODEL
    t = lax.broadcasted_iota(jnp.int32, (x_ref.shape[0], 1), 0) % SUBLANES
    v, h1 = _conv_layer(x_ref[...], p1_ref[...], p2_ref[...], t == 0, t < 2, w)
    cv_ref[...] = v
    h1_ref[...] = h1
    hn = _rms(h1, w['kv_norm'][...]).astype(BF16)
    kvT = w['kv_wT']
    k_ref[...] = _head_rms(_dot_nt(hn, kvT[0:D, :]), w['k_norm_row'][...])
    v_ref[...] = _dot_nt(hn, kvT[D:2 * D, :])
    lf_ref[...] = _log_sigmoid(_dot_nt(hn, kvT[2 * D + LANES:2 * D + LANES + N_HEADS, :]) + w['fbias_row'][...])
    q, gate = _fox_proj(h1, w)
    q_ref[...] = q
    gate_ref[...] = gate


def _pre_prompt(x, weights):
    B, S, D = x.shape
    T = PRE_T
    tok = lambda d: pl.BlockSpec((1, T, d), lambda b, j: (b, j, 0))
    feat = lambda n: pl.BlockSpec((1, n, T), lambda b, j: (b, 0, j))
    out_shape = [
        jax.ShapeDtypeStruct((B, S, D), F32),
        jax.ShapeDtypeStruct((B, D, S), F32),
        jax.ShapeDtypeStruct((B, D, S), F32),
        jax.ShapeDtypeStruct((B, N_HEADS, S), F32),
        jax.ShapeDtypeStruct((B, D, S), BF16),
        jax.ShapeDtypeStruct((B, LANES, S), BF16),
        jax.ShapeDtypeStruct((B, LANES, S), BF16),
        jax.ShapeDtypeStruct((B, D, S), BF16),
        jax.ShapeDtypeStruct((B, SUBLANES, D), F32),
    ]
    out_specs = [tok(D), feat(D), feat(D), feat(N_HEADS), feat(D), feat(LANES), feat(LANES), feat(D),
                 pl.BlockSpec((1, SUBLANES, D), lambda b, j: (b, 0, 0))]
    const = lambda b, j: (0, 0)
    w_specs = [pl.BlockSpec(w.shape, const, pipeline_mode=pl.Buffered(1)) for w in weights]
    return pl.pallas_call(
        _pre_prompt_kernel,
        grid=(B, S // T),
        in_specs=[tok(D)] + w_specs,
        out_specs=out_specs,
        out_shape=out_shape,
        scratch_shapes=[pltpu.VMEM((SUBLANES, D), F32), pltpu.VMEM((LANES, LANES), F32)],
        compiler_params=pltpu.CompilerParams(
            dimension_semantics=("arbitrary", "arbitrary"), vmem_limit_bytes=VMEM_LIMIT),
        name="pre_prompt",
    )(x, *weights)


def _pre_sample(x2, p1, p2, weights):
    R, D = x2.shape
    out_shape = [
        jax.ShapeDtypeStruct((R, D), F32),
        jax.ShapeDtypeStruct((R, D), F32),
        jax.ShapeDtypeStruct((R, D), F32),
        jax.ShapeDtypeStruct((R, N_HEADS), F32),
        jax.ShapeDtypeStruct((R, D), BF16),
        jax.ShapeDtypeStruct((R, D), BF16),
        jax.ShapeDtypeStruct((R, D), F32),
    ]
    return pl.pallas_call(
        _pre_sample_kernel,
        out_shape=out_shape,
        compiler_params=pltpu.CompilerParams(vmem_limit_bytes=VMEM_LIMIT),
        name="pre_sample",
    )(x2, p1, p2, *weights)


def _prompt_attn_step(hp, j, qa_ref, qb_ref, qfa_ref, qfb_ref, kT_ref, kfT_ref, vT_ref, ga_ref, gb_ref,
                      o_ref, kbuf, vbuf, alongside):
    T = qa_ref.shape[2]
    n = kbuf.shape[0]

    @pl.when(j == 0)
    def _():
        for kt in range(n):
            cols = slice(kt * T, (kt + 1) * T)
            kbuf[kt, :, 0:LANES] = kT_ref[0, :, cols].T.astype(BF16)
            kbuf[kt, :, LANES:2 * LANES] = kfT_ref[0, :, cols].astype(F32).T.astype(BF16)
            vbuf[kt] = vT_ref[0, :, cols].astype(BF16)

    row = lax.broadcasted_iota(jnp.int32, (LANES, 1), 0)
    lo_half = row < HEAD_DIM

    def query_operand(q_ref, qf_ref):
        q2 = q_ref[0]
        qf = qf_ref[0]
        zero = jnp.zeros_like(q2)
        return jnp.concatenate([
            jnp.concatenate([jnp.where(lo_half, q2, zero), jnp.where(row // AUG == 2 * hp, qf, zero)], axis=0),
            jnp.concatenate([jnp.where(lo_half, zero, q2), jnp.where(row // AUG == 2 * hp + 1, qf, zero)], axis=0),
        ], axis=1)

    def scores(q_ref, qf_ref, n_tiles):
        rhs = query_operand(q_ref, qf_ref)
        ss = [_dot(kbuf[kt], rhs) for kt in range(n_tiles)]
        r = lax.broadcasted_iota(jnp.int32, ss[-1].shape, 0)
        c = lax.broadcasted_iota(jnp.int32, ss[-1].shape, 1)
        ss[-1] = jnp.where(r <= jnp.where(c >= T, c - T, c), ss[-1], NEG_BIG)
        m = None
        for s in ss:
            ms = jnp.max(s, axis=0, keepdims=True)
            m = ms if m is None else jnp.maximum(m, ms)
        return ss, m

    def weigh(ss, m, g_ref, half):
        acc = l = None
        for kt, s in enumerate(ss):
            p = jnp.exp2(s - m)
            ps = jnp.sum(p, axis=0, keepdims=True)
            l = ps if l is None else l + ps
            d = _dot(vbuf[kt], p.astype(BF16))
            acc = d if acc is None else acc + d
        o = acc / l
        o = jnp.where(lo_half, o[:, 0:T], o[:, T:2 * T])
        o_ref[half, 0] = (o * g_ref[0].astype(F32)).astype(BF16)

    for v in range(n // 2):
        @pl.when(j == v)
        def _(v=v):
            sa = scores(qa_ref, qfa_ref, v + 1)
            sb = scores(qb_ref, qfb_ref, n - v)
            side = alongside[0]()
            weigh(*sa, ga_ref, 0)
            alongside[1](side)
            weigh(*sb, gb_ref, 1)


def _sample_attn_parts(refs):
    P = PAGES_PER_STEP
    sub = P // 2
    kpages = refs[0:P]
    vpages = refs[P:2 * P]
    lpages = refs[2 * P:3 * P]
    (knew_ref, vnew_ref, lnew_ref, q_ref, gate_ref, tri_ref,
     o_ref, qblk, gcarry, m_scr, l_scr, acc_scr) = refs[3 * P:]
    nq = q_ref.shape[1]
    R = LANES
    PG = LANES
    D = D_MODEL

    def begin():
        q = q_ref[0]
        rows = lax.broadcasted_iota(jnp.int32, (R, D), 0)
        cols = lax.broadcasted_iota(jnp.int32, (R, D), 1)
        tiled = jnp.concatenate([q] * (R // nq), axis=0)
        qblk[...] = jnp.where(cols // HEAD_DIM == rows // nq, tiled, jnp.zeros_like(tiled))
        gcarry[...] = jnp.zeros_like(gcarry)
        m_scr[...] = jnp.full_like(m_scr, NEG_BIG)
        l_scr[...] = jnp.zeros_like(l_scr)
        acc_scr[...] = jnp.zeros_like(acc_scr)

    def bias_rows(lf_pages):
        cs = _dot_exact_lhs(jnp.concatenate(lf_pages, axis=0), tri_ref[...])
        carry = gcarry[...]
        out = []
        for j in range(len(lf_pages)):
            G = cs[j * N_HEADS:(j + 1) * N_HEADS] + carry
            carry = jnp.broadcast_to(G[:, PG - 1:PG], carry.shape)
            out.append(jnp.concatenate(
                [jnp.broadcast_to(G[h:h + 1], (nq, PG)) for h in range(N_HEADS)], axis=0))
        gcarry[...] = carry
        return jnp.concatenate(out, axis=1) if len(out) > 1 else out[0]

    def update(s, vT):
        m_prev = m_scr[...]
        m_new = jnp.maximum(m_prev, jnp.max(s, axis=1, keepdims=True))
        alpha = jnp.exp2(m_prev - m_new)
        p = jnp.exp2(s - _tile_lanes(m_new, s.shape[1] // LANES))
        l_scr[...] = alpha * l_scr[...] + jnp.sum(p, axis=1, keepdims=True)
        acc_scr[...] = jnp.concatenate([alpha.T] * (D // LANES), axis=0) * acc_scr[...] + _dot_nt(vT, p.astype(BF16))
        m_scr[...] = m_new

    def chunk_scores():
        bias = bias_rows([lpages[j][0] for j in range(P)]) * LOG2E
        subs = []
        for h0 in range(0, P, sub):
            kT = jnp.concatenate([kpages[j][0].astype(BF16) for j in range(h0, h0 + sub)], axis=1)
            subs.append(_dot(qblk[...], kT) - bias[:, h0 * PG:(h0 + sub) * PG])
        return subs

    def chunk_update(subs):
        for h0, s in zip(range(0, P, sub), subs):
            update(s, jnp.concatenate([vpages[j][0].astype(BF16) for j in range(h0, h0 + sub)], axis=1))

    def finish():
        pad = jnp.zeros((PG - nq, D), F32)
        k_new = jnp.concatenate([knew_ref[0], pad], axis=0).astype(BF16)
        v_new = jnp.concatenate([vnew_ref[0], pad], axis=0).astype(BF16)
        s = _dot_nt(qblk[...], k_new) - bias_rows([lnew_ref[0]]) * LOG2E
        key = lax.broadcasted_iota(jnp.int32, s.shape, 1)
        qi = lax.broadcasted_iota(jnp.int32, s.shape, 0) % nq
        s = jnp.where(key <= qi, s, NEG_BIG)
        m_prev = m_scr[...]
        m_new = jnp.maximum(m_prev, jnp.max(s, axis=1, keepdims=True))
        alpha = jnp.exp2(m_prev - m_new)
        p = jnp.exp2(s - m_new)
        linv = 1.0 / (alpha * l_scr[...] + jnp.sum(p, axis=1, keepdims=True))

        oT = acc_scr[...] * jnp.concatenate([(alpha * linv).T] * (D // LANES), axis=0)
        rows = lax.broadcasted_iota(jnp.int32, oT.shape, 0)
        cols = lax.broadcasted_iota(jnp.int32, oT.shape, 1)
        oT = jnp.where(rows // HEAD_DIM == cols // nq, oT, 0.0)
        shift = nq
        while shift < LANES:
            oT = oT + pltpu.roll(oT, shift, axis=1)
            shift *= 2
        o = oT.T[0:nq]

        o2 = _dot(p.astype(BF16), v_new) * _tile_lanes(linv, D // LANES)
        r2 = lax.broadcasted_iota(jnp.int32, o2.shape, 0)
        c2 = lax.broadcasted_iota(jnp.int32, o2.shape, 1)
        o2 = jnp.where(c2 // HEAD_DIM == r2 // nq, o2, 0.0)
        for h in range(N_HEADS):
            o = o + o2[h * nq:(h + 1) * nq]
        o_ref[0] = (o * gate_ref[0].astype(F32)).astype(BF16)

    return begin, (chunk_scores, chunk_update), finish


N_PROMPT_IN = 9


def _attn_kernel(pt_ref, *refs):
    del pt_ref
    n_sample_in = 3 * PAGES_PER_STEP + 6
    prompt_in = refs[:N_PROMPT_IN]
    sample_in = refs[N_PROMPT_IN:N_PROMPT_IN + n_sample_in]
    o_ref, os_ref, kbuf, vbuf, qblk, gcarry, m_scr, l_scr, acc_scr = refs[N_PROMPT_IN + n_sample_in:]
    hp = pl.program_id(1)
    j = pl.program_id(2)
    nj = kbuf.shape[0] // 2
    n_chunks = (D_MODEL // LANES) * nj // SEQS_PER_BATCH_STEP
    c = (hp * nj + j) % n_chunks
    begin, chunk, finish = _sample_attn_parts(sample_in + (os_ref, qblk, gcarry, m_scr, l_scr, acc_scr))
    pl.when(c == 0)(begin)
    _prompt_attn_step(hp, j, *prompt_in, o_ref, kbuf, vbuf, chunk)
    pl.when(c == n_chunks - 1)(finish)


def _attention(qT, qfT, kT, kfT, vT, gateT, page_table, ckT, cvT, clT, knew, vnew, lnewT, qs, gates, tri_u):
    B, D, S = qT.shape
    NB, NQ, _ = qs.shape
    T = ATT_T
    n = S // T
    HP, NJ = D // LANES, n // 2
    P = PAGES_PER_STEP
    PG = ckT.shape[2]
    n_chunks = page_table.shape[1] // P
    assert HP * NJ == SEQS_PER_BATCH_STEP * n_chunks and NB == B * SEQS_PER_BATCH_STEP

    lo = lambda rows: pl.BlockSpec((1, LANES, T), lambda b, hp, j, pt: (b, hp if rows else 0, j))
    hi = lambda rows: pl.BlockSpec((1, LANES, T), lambda b, hp, j, pt: (b, hp if rows else 0, n - 1 - j))
    kspec = pl.BlockSpec((1, LANES, S), lambda b, hp, j, pt: (b, hp, 0))
    prompt_specs = [lo(True), hi(True), lo(False), hi(False),
                    kspec, pl.BlockSpec((1, LANES, S), lambda b, hp, j, pt: (b, 0, 0)), kspec, lo(True), hi(True)]

    seq = lambda b, hp, j: b * SEQS_PER_BATCH_STEP + (hp * NJ + j) // n_chunks

    def page_spec(rows, jj):
        return pl.BlockSpec(
            (1, rows, PG),
            lambda b, hp, j, pt: (pt[seq(b, hp, j), ((hp * NJ + j) % n_chunks) * P + jj], 0, 0))

    per_seq = lambda rows, width: pl.BlockSpec((1, rows, width), lambda b, hp, j, pt: (seq(b, hp, j), 0, 0))
    sample_specs = ([page_spec(D, jj) for jj in range(P)] + [page_spec(D, jj) for jj in range(P)]
                    + [page_spec(N_HEADS, jj) for jj in range(P)]
                    + [per_seq(NQ, D), per_seq(NQ, D), per_seq(N_HEADS, PG), per_seq(NQ, D), per_seq(NQ, D),
                       pl.BlockSpec(tri_u.shape, lambda b, hp, j, pt: (0, 0))])
    assert len(prompt_specs) == N_PROMPT_IN
    grid_spec = pltpu.PrefetchScalarGridSpec(
        num_scalar_prefetch=1,
        grid=(B, HP, NJ),
        in_specs=prompt_specs + sample_specs,
        out_specs=[pl.BlockSpec((2, 1, LANES, T), lambda b, hp, j, pt: (0, b, hp, j)), per_seq(NQ, D)],
        scratch_shapes=[pltpu.VMEM((n, T, 2 * LANES), BF16), pltpu.VMEM((n, LANES, T), BF16),
                        pltpu.VMEM((LANES, D), BF16),
                        pltpu.VMEM((N_HEADS, LANES), F32),
                        pltpu.VMEM((LANES, LANES), F32), pltpu.VMEM((LANES, LANES), F32),
                        pltpu.VMEM((D, LANES), F32)],
    )
    return pl.pallas_call(
        _attn_kernel,
        grid_spec=grid_spec,
        out_shape=[jax.ShapeDtypeStruct((2, B, D, S // 2), BF16), jax.ShapeDtypeStruct((NB, NQ, D), BF16)],
        compiler_params=pltpu.CompilerParams(
            dimension_semantics=("arbitrary", "arbitrary", "arbitrary"), vmem_limit_bytes=VMEM_LIMIT),
        name="attn",
    )(page_table, qT, qT, qfT, qfT, kT, kfT, vT, gateT, gateT,
      *([ckT] * P), *([cvT] * P), *([clT] * P), knew, vnew, lnewT, qs, gates, tri_u)


def _outproj_fm_kernel(ogT_ref, h_ref, w_ref, y_ref):
    T = ATT_T
    og = ogT_ref[0, 0]
    swapped = pl.program_id(1) >= pl.num_programs(1) // 2
    contract = lambda a: lax.dot_general(a, w_ref[...], (((0,), (0,)), ((), ())), preferred_element_type=F32)
    y_ref[0, 0:T] = h_ref[0, 0:T] + contract(jnp.where(swapped, og[:, T:2 * T], og[:, 0:T]))
    y_ref[0, T:2 * T] = h_ref[0, T:2 * T] + contract(jnp.where(swapped, og[:, 0:T], og[:, T:2 * T]))


def _outproj_fm(og2, h, w):
    _, B, D, S2 = og2.shape
    T = 2 * ATT_T
    n = 2 * S2 // T
    og_map = lambda b, t: (t // (n // 2), b, 0, jnp.where(t < n // 2, t, n - 1 - t))
    return pl.pallas_call(
        _outproj_fm_kernel,
        grid=(B, n),
        in_specs=[pl.BlockSpec((1, 1, D, T), og_map), pl.BlockSpec((1, T, D), lambda b, t: (b, t, 0)),
                  pl.BlockSpec(w.shape, lambda b, t: (0, 0), pipeline_mode=pl.Buffered(1))],
        out_specs=pl.BlockSpec((1, T, D), lambda b, t: (b, t, 0)),
        out_shape=jax.ShapeDtypeStruct((B, 2 * S2, D), F32),
        compiler_params=pltpu.CompilerParams(
            dimension_semantics=("arbitrary", "arbitrary"), vmem_limit_bytes=VMEM_LIMIT),
        name="outproj_prompt",
    )(og2, h, w)


def _outproj_kernel(og_ref, h_ref, w_ref, y_ref):
    y_ref[...] = h_ref[...] + _dot(og_ref[...], w_ref[...])


def _outproj(og, h, w):
    R, D = h.shape
    T = min(OUT_T, R)
    row = pl.BlockSpec((T, D), lambda i: (i, 0))
    return pl.pallas_call(
        _outproj_kernel,
        grid=(R // T,),
        in_specs=[row, row, pl.BlockSpec(w.shape, lambda i: (0, 0), pipeline_mode=pl.Buffered(1))],
        out_specs=row,
        out_shape=jax.ShapeDtypeStruct((R, D), F32),
        compiler_params=pltpu.CompilerParams(
            dimension_semantics=("arbitrary",), vmem_limit_bytes=VMEM_LIMIT),
        name="outproj",
    )(og, h, w)


def _upper_tri(n):
    return (jnp.arange(n)[:, None] <= jnp.arange(n)[None, :]).astype(BF16)


def kernel(x_prompt, x_sample, state_conv, cache_k, cache_v, cache_logf, page_table,
           a_norm, a_w_in, a_conv, a_w_out, kv_norm, kv_w, kv_fbias, k_norm,
           b_norm, b_w_in, q_norm, b_w_out):
    B, S, D = x_prompt.shape
    NB, NQ, _ = x_sample.shape
    n_pool, PG = cache_k.shape[0], cache_k.shape[1]
    H, hd = N_HEADS, HEAD_DIM
    assert a_norm.shape[0] == 1 and b_norm.shape[0] == 1, "one conv layer then one attention layer"
    assert D == D_MODEL and NQ == SUBLANES and PG == LANES

    common = [a_norm[0][None], a_w_in[0].astype(BF16), a_conv[0], a_w_out[0].astype(BF16),
              kv_norm[None], b_norm[0][None]]
    b_w_in_b = b_w_in[0].astype(BF16)
    kv_wT = kv_w.T
    kv_wT_ext = jnp.concatenate([kv_wT[:2 * D], jnp.repeat(kv_wT[2 * D:], AUG, axis=0), kv_wT[2 * D:]], axis=0)
    fbias_col = jnp.broadcast_to(jnp.concatenate([jnp.repeat(kv_fbias, AUG), kv_fbias])[:, None],
                                 (LANES + H, LANES))
    k_norm_col = jnp.broadcast_to(jnp.tile(k_norm, H)[:, None], (D, LANES))
    q_norm_col = jnp.broadcast_to(jnp.tile(q_norm[0], H)[:, None], (D, LANES))
    kv_wT_b = kv_wT_ext.astype(BF16)
    w_prompt = common + [kv_wT_b, fbias_col, k_norm_col, _upper_tri(PRE_T), b_w_in_b.T, q_norm_col]
    w_sample = common + [kv_wT_b, kv_fbias[None], jnp.tile(k_norm, 2)[None], b_w_in_b, jnp.tile(q_norm[0], 2)[None]]
    w_out_b = b_w_out[0].astype(BF16)

    (h1, kT, vT, lfT, qT, qfT, kfT, gateT, tail) = _pre_prompt(x_prompt, w_prompt)
    st = state_conv[0]
    p1 = jnp.pad(st[:, 1:2], ((0, 0), (0, NQ - 1), (0, 0))).reshape(NB * NQ, D)
    p2 = jnp.pad(st, ((0, 0), (0, NQ - 2), (0, 0))).reshape(NB * NQ, D)
    (h1s, k_s, v_s, lf_s, qs, gate_s, cv_s) = _pre_sample(x_sample.reshape(NB * NQ, D), p1, p2, w_sample)

    lnewT = jnp.pad(lf_s.reshape(NB, NQ, H), ((0, 0), (0, PG - NQ), (0, 0))).transpose(0, 2, 1)
    og2, ogs = _attention(
        qT, qfT, kT, kfT, vT, gateT, page_table,
        cache_k.transpose(0, 2, 3, 1).reshape(n_pool, D, PG),
        cache_v.transpose(0, 2, 3, 1).reshape(n_pool, D, PG),
        cache_logf.transpose(0, 2, 1),
        k_s.reshape(NB, NQ, D), v_s.reshape(NB, NQ, D), lnewT,
        qs.reshape(NB, NQ, D), gate_s.reshape(NB, NQ, D), _upper_tri(PG))
    y_prompt = _outproj_fm(og2, h1, w_out_b)
    y_sample = _outproj(ogs.reshape(NB * NQ, D), h1s, w_out_b).reshape(NB, NQ, D)

    conv_prompt = tail[:, SUBLANES - (CONV_W - 1):][None]
    conv_sample = cv_s.reshape(NB, NQ, D)[:, NQ - (CONV_W - 1):][None]
    to_bshd = lambda t: t.reshape(B, H, hd, S).transpose(0, 3, 1, 2)
    return (y_prompt, y_sample, conv_prompt, conv_sample,
            to_bshd(kT), to_bshd(vT), lfT.transpose(0, 2, 1),
            k_s.reshape(NB, NQ, H, hd), v_s.reshape(NB, NQ, H, hd), lf_s.reshape(NB, NQ, H))
```

```python
import jax
import jax.numpy as jnp
from jax import lax
from jax.experimental import pallas as pl
from jax.experimental.pallas import tpu as pltpu

F32 = jnp.float32
BF16 = jnp.bfloat16

D_MODEL = 1024
N_HEADS = 16
HEAD_DIM = 64
CONV_W = 3
EPS = 1e-6
LOG2E = 1.4426950408889634
NEG_BIG = -1e30

LANES = 128
SUBLANES = 8
AUG = LANES // N_HEADS
VMEM_LIMIT = 56 * 1024 * 1024

PRE_T = 512
ATT_T = 256
OUT_T = 512
PAGES_PER_STEP = 8
SEQS_PER_BATCH_STEP = 4


def _dot(a, b):
    return jnp.dot(a, b, preferred_element_type=F32)


def _dot_nt(a, b):
    return lax.dot_general(a, b, (((1,), (1,)), ((), ())), preferred_element_type=F32)


def _tile_lanes(x, n):
    return jnp.concatenate([x] * n, axis=1) if n > 1 else x


def _split3(x):
    hi = x.astype(BF16)
    r1 = x - hi.astype(F32)
    mid = r1.astype(BF16)
    r2 = r1 - mid.astype(F32)
    return hi, mid, r2.astype(BF16)


def _dot_exact_rhs(m_bf16, x_f32):
    hi, mid, lo = _split3(x_f32)
    return _dot(m_bf16, hi) + _dot(m_bf16, mid) + _dot(m_bf16, lo)


def _dot_exact_lhs(x_f32, m_bf16):
    hi, mid, lo = _split3(x_f32)
    return _dot(hi, m_bf16) + _dot(mid, m_bf16) + _dot(lo, m_bf16)


def _rms(x, g):
    r = lax.rsqrt(jnp.mean(x * x, axis=-1, keepdims=True) + EPS)
    return x * r * g


def _head_rms(x, g128):
    lane = lax.broadcasted_iota(jnp.int32, (1, LANES), 1)
    lo_half = lane < HEAD_DIM
    outs = []
    for p in range(x.shape[1] // LANES):
        xp = x[:, p * LANES:(p + 1) * LANES]
        sq = xp * xp
        s_lo = jnp.sum(jnp.where(lo_half, sq, 0.0), axis=-1, keepdims=True)
        s_hi = jnp.sum(jnp.where(lo_half, 0.0, sq), axis=-1, keepdims=True)
        r = jnp.where(lo_half, lax.rsqrt(s_lo / HEAD_DIM + EPS), lax.rsqrt(s_hi / HEAD_DIM + EPS))
        outs.append(xp * r * g128)
    return jnp.concatenate(outs, axis=1)


def _head_rms_fm(xT, gT):
    n, T = xT.shape
    x3 = xT.reshape(n // HEAD_DIM, HEAD_DIM, T)
    r = lax.rsqrt(jnp.mean(x3 * x3, axis=1, keepdims=True) + EPS)
    return (x3 * r).reshape(n, T) * gT


def _silu(z):
    return z * jax.nn.sigmoid(z)


def _log_sigmoid(x):
    return jnp.minimum(x, 0.0) - jnp.log1p(jnp.exp(-jnp.abs(x)))


def _bias_select(F, j, key_side):
    hi, mid, lo = (t.astype(F32) for t in _split3(F))
    if key_side:
        out = jnp.where(j < 3, 1.0, jnp.where(j == 3, -hi, jnp.where(j == 4, -mid, jnp.where(j == 5, -lo, 0.0))))
    else:
        out = jnp.where(j == 0, hi, jnp.where(j == 1, mid, jnp.where(j == 2, lo, jnp.where(j < 6, 1.0, 0.0))))
    return out.astype(BF16)


def _conv_layer(x, prev1, prev2, m1, m2, w):
    D = D_MODEL
    xn = _rms(x, w['a_norm'][...]).astype(BF16)
    w_in = w['a_w_in']
    v = _dot(xn, w_in[:, D:2 * D]) * _dot(xn, w_in[:, 2 * D:3 * D])
    v1 = jnp.where(m1, prev1, pltpu.roll(v, 1, axis=0))
    v2 = jnp.where(m2, prev2, pltpu.roll(v, 2, axis=0))
    cw = w['a_conv'][...]
    conv = v2 * cw[0:1] + v1 * cw[1:2] + v * cw[2:3]
    b = _dot(xn, w_in[:, 0:D])
    z = _dot(xn, w_in[:, 3 * D:4 * D])
    g = (_silu(z) * b * conv).astype(BF16)
    return v, x + _dot(g, w['a_w_out'][...])


def _fox_proj(h1, w):
    D = D_MODEL
    xn = _rms(h1, w['b_norm'][...]).astype(BF16)
    b_w_in = w['b_w_in']
    q = _head_rms(_dot(xn, b_w_in[:, 0:D]), w['q_norm'][...]) * (LOG2E * HEAD_DIM ** -0.5)
    gate = _silu(_dot(xn, b_w_in[:, D:2 * D]))
    return q.astype(BF16), gate.astype(BF16)


_W_COMMON = ('a_norm', 'a_w_in', 'a_conv', 'a_w_out', 'kv_norm', 'b_norm')
_W_PROMPT = _W_COMMON + ('kv_wT', 'fbias_col', 'k_norm_col', 'tri_u', 'b_w_inT', 'q_norm_col')
_W_SAMPLE = _W_COMMON + ('kv_wT', 'fbias_row', 'k_norm_row', 'b_w_in', 'q_norm')


def _pre_prompt_kernel(x_ref, *refs):
    nw = len(_W_PROMPT)
    w = dict(zip(_W_PROMPT, refs[:nw]))
    (h1_ref, kT_ref, vT_ref, lfT_ref, qT_ref, qfT_ref, kfT_ref, gateT_ref, tail_ref, vcarry, fcarry) = refs[nw:]
    T = x_ref.shape[1]
    D = D_MODEL
    reps = T // LANES

    @pl.when(pl.program_id(1) == 0)
    def _():
        vcarry[...] = jnp.zeros_like(vcarry)
        fcarry[...] = jnp.zeros_like(fcarry)

    row = lax.broadcasted_iota(jnp.int32, (T, 1), 0)
    prev = vcarry[...]
    pm1 = prev[SUBLANES - 1:SUBLANES]
    pm2 = prev[SUBLANES - 2:SUBLANES - 1]
    v, h1 = _conv_layer(x_ref[0], pm1, jnp.where(row == 0, pm2, pm1), row == 0, row < 2, w)
    tail = v[T - SUBLANES:T]
    vcarry[...] = tail
    tail_ref[0] = tail
    h1_ref[0] = h1

    hn = _rms(h1, w['kv_norm'][...]).astype(BF16)
    kvT = w['kv_wT']
    kT_ref[0] = _head_rms_fm(_dot_nt(kvT[0:D, :], hn), _tile_lanes(w['k_norm_col'][...], reps))
    vT_ref[0] = _dot_nt(kvT[D:2 * D, :], hn)
    fT = _dot_nt(kvT[2 * D:2 * D + LANES + N_HEADS, :], hn) + _tile_lanes(w['fbias_col'][...], reps)
    lfT = _log_sigmoid(fT)
    lfT_ref[0] = lfT[LANES:LANES + N_HEADS]
    FT = _dot_exact_lhs(lfT[0:LANES], w['tri_u'][...]) + _tile_lanes(fcarry[...], reps)
    fcarry[...] = jnp.broadcast_to(FT[:, T - 1:T], fcarry.shape)
    Fp = FT * LOG2E
    slot = lax.broadcasted_iota(jnp.int32, (LANES, 1), 0) % AUG
    kfT_ref[0] = _bias_select(Fp, slot, True)
    qfT_ref[0] = _bias_select(Fp, slot, False)

    xn2 = _rms(h1, w['b_norm'][...]).astype(BF16)
    b_w_inT = w['b_w_inT']
    qT = _head_rms_fm(_dot_nt(b_w_inT[0:D, :], xn2), _tile_lanes(w['q_norm_col'][...], reps))
    qT_ref[0] = (qT * (LOG2E * HEAD_DIM ** -0.5)).astype(BF16)
    gateT_ref[0] = _silu(_dot_nt(b_w_inT[D:2 * D, :], xn2)).astype(BF16)


def _pre_sample_kernel(x_ref, p1_ref, p2_ref, *refs):
    nw = len(_W_SAMPLE)
    w = dict(zip(_W_SAMPLE, refs[:nw]))
    h1_ref, k_ref, v_ref, lf_ref, q_ref, gate_ref, cv_ref = refs[nw:]
    D = D_MODEL
    t = lax.broadcasted_iota(jnp.int32, (x_ref.shape[0], 1), 0) % SUBLANES
    v, h1 = _conv_layer(x_ref[...], p1_ref[...], p2_ref[...], t == 0, t < 2, w)
    cv_ref[...] = v
    h1_ref[...] = h1
    hn = _rms(h1, w['kv_norm'][...]).astype(BF16)
    kvT = w['kv_wT']
    k_ref[...] = _head_rms(_dot_nt(hn, kvT[0:D, :]), w['k_norm_row'][...])
    v_ref[...] = _dot_nt(hn, kvT[D:2 * D, :])
    lf_ref[...] = _log_sigmoid(_dot_nt(hn, kvT[2 * D + LANES:2 * D + LANES + N_HEADS, :]) + w['fbias_row'][...])
    q, gate = _fox_proj(h1, w)
    q_ref[...] = q
    gate_ref[...] = gate


def _pre_prompt(x, weights):
    B, S, D = x.shape
    T = PRE_T
    tok = lambda d: pl.BlockSpec((1, T, d), lambda b, j: (b, j, 0))
    feat = lambda n: pl.BlockSpec((1, n, T), lambda b, j: (b, 0, j))
    out_shape = [
        jax.ShapeDtypeStruct((B, S, D), F32),
        jax.ShapeDtypeStruct((B, D, S), F32),
        jax.ShapeDtypeStruct((B, D, S), F32),
        jax.ShapeDtypeStruct((B, N_HEADS, S), F32),
        jax.ShapeDtypeStruct((B, D, S), BF16),
        jax.ShapeDtypeStruct((B, LANES, S), BF16),
        jax.ShapeDtypeStruct((B, LANES, S), BF16),
        jax.ShapeDtypeStruct((B, D, S), BF16),
        jax.ShapeDtypeStruct((B, SUBLANES, D), F32),
    ]
    out_specs = [tok(D), feat(D), feat(D), feat(N_HEADS), feat(D), feat(LANES), feat(LANES), feat(D),
                 pl.BlockSpec((1, SUBLANES, D), lambda b, j: (b, 0, 0))]
    const = lambda b, j: (0, 0)
    w_specs = [pl.BlockSpec(w.shape, const, pipeline_mode=pl.Buffered(1)) for w in weights]
    return pl.pallas_call(
        _pre_prompt_kernel,
        grid=(B, S // T),
        in_specs=[tok(D)] + w_specs,
        out_specs=out_specs,
        out_shape=out_shape,
        scratch_shapes=[pltpu.VMEM((SUBLANES, D), F32), pltpu.VMEM((LANES, LANES), F32)],
        compiler_params=pltpu.CompilerParams(
            dimension_semantics=("arbitrary", "arbitrary"), vmem_limit_bytes=VMEM_LIMIT),
        name="pre_prompt",
    )(x, *weights)


def _pre_sample(x2, p1, p2, weights):
    R, D = x2.shape
    out_shape = [
        jax.ShapeDtypeStruct((R, D), F32),
        jax.ShapeDtypeStruct((R, D), F32),
        jax.ShapeDtypeStruct((R, D), F32),
        jax.ShapeDtypeStruct((R, N_HEADS), F32),
        jax.ShapeDtypeStruct((R, D), BF16),
        jax.ShapeDtypeStruct((R, D), BF16),
        jax.ShapeDtypeStruct((R, D), F32),
    ]
    return pl.pallas_call(
        _pre_sample_kernel,
        out_shape=out_shape,
        compiler_params=pltpu.CompilerParams(vmem_limit_bytes=VMEM_LIMIT),
        name="pre_sample",
    )(x2, p1, p2, *weights)


def _prompt_attn_step(hp, j, qa_ref, qb_ref, qfa_ref, qfb_ref, kT_ref, kfT_ref, vT_ref, ga_ref, gb_ref,
                      o_ref, kbuf, vbuf, alongside):
    T = qa_ref.shape[2]
    n = kbuf.shape[0]

    @pl.when(j == 0)
    def _():
        for kt in range(n):
            cols = slice(kt * T, (kt + 1) * T)
            kbuf[kt, :, 0:LANES] = kT_ref[0, :, cols].T.astype(BF16)
            kbuf[kt, :, LANES:2 * LANES] = kfT_ref[0, :, cols].astype(F32).T.astype(BF16)
            vbuf[kt] = vT_ref[0, :, cols].astype(BF16)

    row = lax.broadcasted_iota(jnp.int32, (LANES, 1), 0)
    lo_half = row < HEAD_DIM

    def query_operand(q_ref, qf_ref):
        q2 = q_ref[0]
        qf = qf_ref[0]
        zero = jnp.zeros_like(q2)
        return jnp.concatenate([
            jnp.concatenate([jnp.where(lo_half, q2, zero), jnp.where(row // AUG == 2 * hp, qf, zero)], axis=0),
            jnp.concatenate([jnp.where(lo_half, zero, q2), jnp.where(row // AUG == 2 * hp + 1, qf, zero)], axis=0),
        ], axis=1)

    def scores(q_ref, qf_ref, n_tiles):
        rhs = query_operand(q_ref, qf_ref)
        ss = [_dot(kbuf[kt], rhs) for kt in range(n_tiles)]
        r = lax.broadcasted_iota(jnp.int32, ss[-1].shape, 0)
        c = lax.broadcasted_iota(jnp.int32, ss[-1].shape, 1)
        ss[-1] = jnp.where(r <= jnp.where(c >= T, c - T, c), ss[-1], NEG_BIG)
        m = None
        for s in ss:
            ms = jnp.max(s, axis=0, keepdims=True)
            m = ms if m is None else jnp.maximum(m, ms)
        return ss, m

    def weigh(ss, m, g_ref, half):
        acc = l = None
        for kt, s in enumerate(ss):
            p = jnp.exp2(s - m)
            ps = jnp.sum(p, axis=0, keepdims=True)
            l = ps if l is None else l + ps
            d = _dot(vbuf[kt], p.astype(BF16))
            acc = d if acc is None else acc + d
        o = acc / l
        o = jnp.where(lo_half, o[:, 0:T], o[:, T:2 * T])
        o_ref[half, 0] = (o * g_ref[0].astype(F32)).astype(BF16)

    for v in range(n // 2):
        @pl.when(j == v)
        def _(v=v):
            sa = scores(qa_ref, qfa_ref, v + 1)
            side = alongside[0]()
            sb = scores(qb_ref, qfb_ref, n - v)
            weigh(*sa, ga_ref, 0)
            alongside[1](side)
            weigh(*sb, gb_ref, 1)


def _sample_attn_parts(refs):
    P = PAGES_PER_STEP
    sub = P // 2
    kpages = refs[0:P]
    vpages = refs[P:2 * P]
    lpages = refs[2 * P:3 * P]
    (knew_ref, vnew_ref, lnew_ref, q_ref, gate_ref, tri_ref,
     o_ref, qblk, gcarry, m_scr, l_scr, acc_scr) = refs[3 * P:]
    nq = q_ref.shape[1]
    R = LANES
    PG = LANES
    D = D_MODEL

    def begin():
        q = q_ref[0]
        rows = lax.broadcasted_iota(jnp.int32, (R, D), 0)
        cols = lax.broadcasted_iota(jnp.int32, (R, D), 1)
        tiled = jnp.concatenate([q] * (R // nq), axis=0)
        qblk[...] = jnp.where(cols // HEAD_DIM == rows // nq, tiled, jnp.zeros_like(tiled))
        gcarry[...] = jnp.zeros_like(gcarry)
        m_scr[...] = jnp.full_like(m_scr, NEG_BIG)
        l_scr[...] = jnp.zeros_like(l_scr)
        acc_scr[...] = jnp.zeros_like(acc_scr)

    def bias_rows(lf_pages):
        cs = _dot_exact_lhs(jnp.concatenate(lf_pages, axis=0), tri_ref[...])
        carry = gcarry[...]
        out = []
        for j in range(len(lf_pages)):
            G = cs[j * N_HEADS:(j + 1) * N_HEADS] + carry
            carry = jnp.broadcast_to(G[:, PG - 1:PG], carry.shape)
            out.append(jnp.concatenate(
                [jnp.broadcast_to(G[h:h + 1], (nq, PG)) for h in range(N_HEADS)], axis=0))
        gcarry[...] = carry
        return jnp.concatenate(out, axis=1) if len(out) > 1 else out[0]

    def update(s, vT):
        m_prev = m_scr[...]
        m_new = jnp.maximum(m_prev, jnp.max(s, axis=1, keepdims=True))
        alpha = jnp.exp2(m_prev - m_new)
        p = jnp.exp2(s - _tile_lanes(m_new, s.shape[1] // LANES))
        l_scr[...] = alpha * l_scr[...] + jnp.sum(p, axis=1, keepdims=True)
        acc_scr[...] = jnp.concatenate([alpha.T] * (D // LANES), axis=0) * acc_scr[...] + _dot_nt(vT, p.astype(BF16))
        m_scr[...] = m_new

    def chunk_scores():
        bias = bias_rows([lpages[j][0] for j in range(P)]) * LOG2E
        subs = []
        for h0 in range(0, P, sub):
            kT = jnp.concatenate([kpages[j][0].astype(BF16) for j in range(h0, h0 + sub)], axis=1)
            subs.append(_dot(qblk[...], kT) - bias[:, h0 * PG:(h0 + sub) * PG])
        return subs

    def chunk_update(subs):
        for h0, s in zip(range(0, P, sub), subs):
            update(s, jnp.concatenate([vpages[j][0].astype(BF16) for j in range(h0, h0 + sub)], axis=1))

    def finish():
        pad = jnp.zeros((PG - nq, D), F32)
        k_new = jnp.concatenate([knew_ref[0], pad], axis=0).astype(BF16)
        v_new = jnp.concatenate([vnew_ref[0], pad], axis=0).astype(BF16)
        s = _dot_nt(qblk[...], k_new) - bias_rows([lnew_ref[0]]) * LOG2E
        key = lax.broadcasted_iota(jnp.int32, s.shape, 1)
        qi = lax.broadcasted_iota(jnp.int32, s.shape, 0) % nq
        s = jnp.where(key <= qi, s, NEG_BIG)
        m_prev = m_scr[...]
        m_new = jnp.maximum(m_prev, jnp.max(s, axis=1, keepdims=True))
        alpha = jnp.exp2(m_prev - m_new)
        p = jnp.exp2(s - m_new)
        linv = 1.0 / (alpha * l_scr[...] + jnp.sum(p, axis=1, keepdims=True))

        oT = acc_scr[...] * jnp.concatenate([(alpha * linv).T] * (D // LANES), axis=0)
        rows = lax.broadcasted_iota(jnp.int32, oT.shape, 0)
        cols = lax.broadcasted_iota(jnp.int32, oT.shape, 1)
        oT = jnp.where(rows // HEAD_DIM == cols // nq, oT, 0.0)
        shift = nq
        while shift < LANES:
            oT = oT + pltpu.roll(oT, shift, axis=1)
            shift *= 2
        o = oT.T[0:nq]

        o2 = _dot(p.astype(BF16), v_new) * _tile_lanes(linv, D // LANES)
        r2 = lax.broadcasted_iota(jnp.int32, o2.shape, 0)
        c2 = lax.broadcasted_iota(jnp.int32, o2.shape, 1)
        o2 = jnp.where(c2 // HEAD_DIM == r2 // nq, o2, 0.0)
        for h in range(N_HEADS):
            o = o + o2[h * nq:(h + 1) * nq]
        o_ref[0] = (o * gate_ref[0].astype(F32)).astype(BF16)

    return begin, (chunk_scores, chunk_update), finish


N_PROMPT_IN = 9


def _attn_kernel(pt_ref, *refs):
    del pt_ref
    n_sample_in = 3 * PAGES_PER_STEP + 6
    prompt_in = refs[:N_PROMPT_IN]
    sample_in = refs[N_PROMPT_IN:N_PROMPT_IN + n_sample_in]
    o_ref, os_ref, kbuf, vbuf, qblk, gcarry, m_scr, l_scr, acc_scr = refs[N_PROMPT_IN + n_sample_in:]
    hp = pl.program_id(1)
    j = pl.program_id(2)
    nj = kbuf.shape[0] // 2
    n_chunks = (D_MODEL // LANES) * nj // SEQS_PER_BATCH_STEP
    c = (hp * nj + j) % n_chunks
    begin, chunk, finish = _sample_attn_parts(sample_in + (os_ref, qblk, gcarry, m_scr, l_scr, acc_scr))
    pl.when(c == 0)(begin)
    _prompt_attn_step(hp, j, *prompt_in, o_ref, kbuf, vbuf, chunk)
    pl.when(c == n_chunks - 1)(finish)


def _attention(qT, qfT, kT, kfT, vT, gateT, page_table, ckT, cvT, clT, knew, vnew, lnewT, qs, gates, tri_u):
    B, D, S = qT.shape
    NB, NQ, _ = qs.shape
    T = ATT_T
    n = S // T
    HP, NJ = D // LANES, n // 2
    P = PAGES_PER_STEP
    PG = ckT.shape[2]
    n_chunks = page_table.shape[1] // P
    assert HP * NJ == SEQS_PER_BATCH_STEP * n_chunks and NB == B * SEQS_PER_BATCH_STEP

    lo = lambda rows: pl.BlockSpec((1, LANES, T), lambda b, hp, j, pt: (b, hp if rows else 0, j))
    hi = lambda rows: pl.BlockSpec((1, LANES, T), lambda b, hp, j, pt: (b, hp if rows else 0, n - 1 - j))
    kspec = pl.BlockSpec((1, LANES, S), lambda b, hp, j, pt: (b, hp, 0))
    prompt_specs = [lo(True), hi(True), lo(False), hi(False),
                    kspec, pl.BlockSpec((1, LANES, S), lambda b, hp, j, pt: (b, 0, 0)), kspec, lo(True), hi(True)]

    seq = lambda b, hp, j: b * SEQS_PER_BATCH_STEP + (hp * NJ + j) // n_chunks

    def page_spec(rows, jj):
        return pl.BlockSpec(
            (1, rows, PG),
            lambda b, hp, j, pt: (pt[seq(b, hp, j), ((hp * NJ + j) % n_chunks) * P + jj], 0, 0))

    per_seq = lambda rows, width: pl.BlockSpec((1, rows, width), lambda b, hp, j, pt: (seq(b, hp, j), 0, 0))
    sample_specs = ([page_spec(D, jj) for jj in range(P)] + [page_spec(D, jj) for jj in range(P)]
                    + [page_spec(N_HEADS, jj) for jj in range(P)]
                    + [per_seq(NQ, D), per_seq(NQ, D), per_seq(N_HEADS, PG), per_seq(NQ, D), per_seq(NQ, D),
                       pl.BlockSpec(tri_u.shape, lambda b, hp, j, pt: (0, 0))])
    assert len(prompt_specs) == N_PROMPT_IN
    grid_spec = pltpu.PrefetchScalarGridSpec(
        num_scalar_prefetch=1,
        grid=(B, HP, NJ),
        in_specs=prompt_specs + sample_specs,
        out_specs=[pl.BlockSpec((2, 1, LANES, T), lambda b, hp, j, pt: (0, b, hp, j)), per_seq(NQ, D)],
        scratch_shapes=[pltpu.VMEM((n, T, 2 * LANES), BF16), pltpu.VMEM((n, LANES, T), BF16),
                        pltpu.VMEM((LANES, D), BF16),
                        pltpu.VMEM((N_HEADS, LANES), F32),
                        pltpu.VMEM((LANES, LANES), F32), pltpu.VMEM((LANES, LANES), F32),
                        pltpu.VMEM((D, LANES), F32)],
    )
    return pl.pallas_call(
        _attn_kernel,
        grid_spec=grid_spec,
        out_shape=[jax.ShapeDtypeStruct((2, B, D, S // 2), BF16), jax.ShapeDtypeStruct((NB, NQ, D), BF16)],
        compiler_params=pltpu.CompilerParams(
            dimension_semantics=("arbitrary", "arbitrary", "arbitrary"), vmem_limit_bytes=VMEM_LIMIT),
        name="attn",
    )(page_table, qT, qT, qfT, qfT, kT, kfT, vT, gateT, gateT,
      *([ckT] * P), *([cvT] * P), *([clT] * P), knew, vnew, lnewT, qs, gates, tri_u)


def _outproj_fm_kernel(ogT_ref, h_ref, w_ref, y_ref):
    T = ATT_T
    og = ogT_ref[0, 0]
    swapped = pl.program_id(1) >= pl.num_programs(1) // 2
    contract = lambda a: lax.dot_general(a, w_ref[...], (((0,), (0,)), ((), ())), preferred_element_type=F32)
    y_ref[0, 0:T] = h_ref[0, 0:T] + contract(jnp.where(swapped, og[:, T:2 * T], og[:, 0:T]))
    y_ref[0, T:2 * T] = h_ref[0, T:2 * T] + contract(jnp.where(swapped, og[:, 0:T], og[:, T:2 * T]))


def _outproj_fm(og2, h, w):
    _, B, D, S2 = og2.shape
    T = 2 * ATT_T
    n = 2 * S2 // T
    og_map = lambda b, t: (t // (n // 2), b, 0, jnp.where(t < n // 2, t, n - 1 - t))
    return pl.pallas_call(
        _outproj_fm_kernel,
        grid=(B, n),
        in_specs=[pl.BlockSpec((1, 1, D, T), og_map), pl.BlockSpec((1, T, D), lambda b, t: (b, t, 0)),
                  pl.BlockSpec(w.shape, lambda b, t: (0, 0), pipeline_mode=pl.Buffered(1))],
        out_specs=pl.BlockSpec((1, T, D), lambda b, t: (b, t, 0)),
        out_shape=jax.ShapeDtypeStruct((B, 2 * S2, D), F32),
        compiler_params=pltpu.CompilerParams(
            dimension_semantics=("arbitrary", "arbitrary"), vmem_limit_bytes=VMEM_LIMIT),
        name="outproj_prompt",
    )(og2, h, w)


def _outproj_kernel(og_ref, h_ref, w_ref, y_ref):
    y_ref[...] = h_ref[...] + _dot(og_ref[...], w_ref[...])


def _outproj(og, h, w):
    R, D = h.shape
    T = min(OUT_T, R)
    row = pl.BlockSpec((T, D), lambda i: (i, 0))
    return pl.pallas_call(
        _outproj_kernel,
        grid=(R // T,),
        in_specs=[row, row, pl.BlockSpec(w.shape, lambda i: (0, 0), pipeline_mode=pl.Buffered(1))],
        out_specs=row,
        out_shape=jax.ShapeDtypeStruct((R, D), F32),
        compiler_params=pltpu.CompilerParams(
            dimension_semantics=("arbitrary",), vmem_limit_bytes=VMEM_LIMIT),
        name="outproj",
    )(og, h, w)


def _upper_tri(n):
    return (jnp.arange(n)[:, None] <= jnp.arange(n)[None, :]).astype(BF16)


def kernel(x_prompt, x_sample, state_conv, cache_k, cache_v, cache_logf, page_table,
           a_norm, a_w_in, a_conv, a_w_out, kv_norm, kv_w, kv_fbias, k_norm,
           b_norm, b_w_in, q_norm, b_w_out):
    B, S, D = x_prompt.shape
    NB, NQ, _ = x_sample.shape
    n_pool, PG = cache_k.shape[0], cache_k.shape[1]
    H, hd = N_HEADS, HEAD_DIM
    assert a_norm.shape[0] == 1 and b_norm.shape[0] == 1, "one conv layer then one attention layer"
    assert D == D_MODEL and NQ == SUBLANES and PG == LANES

    common = [a_norm[0][None], a_w_in[0].astype(BF16), a_conv[0], a_w_out[0].astype(BF16),
              kv_norm[None], b_norm[0][None]]
    b_w_in_b = b_w_in[0].astype(BF16)
    kv_wT = kv_w.T
    kv_wT_ext = jnp.concatenate([kv_wT[:2 * D], jnp.repeat(kv_wT[2 * D:], AUG, axis=0), kv_wT[2 * D:]], axis=0)
    fbias_col = jnp.broadcast_to(jnp.concatenate([jnp.repeat(kv_fbias, AUG), kv_fbias])[:, None],
                                 (LANES + H, LANES))
    k_norm_col = jnp.broadcast_to(jnp.tile(k_norm, H)[:, None], (D, LANES))
    q_norm_col = jnp.broadcast_to(jnp.tile(q_norm[0], H)[:, None], (D, LANES))
    kv_wT_b = kv_wT_ext.astype(BF16)
    w_prompt = common + [kv_wT_b, fbias_col, k_norm_col, _upper_tri(PRE_T), b_w_in_b.T, q_norm_col]
    w_sample = common + [kv_wT_b, kv_fbias[None], jnp.tile(k_norm, 2)[None], b_w_in_b, jnp.tile(q_norm[0], 2)[None]]
    w_out_b = b_w_out[0].astype(BF16)

    (h1, kT, vT, lfT, qT, qfT, kfT, gateT, tail) = _pre_prompt(x_prompt, w_prompt)
    st = state_conv[0]
    p1 = jnp.pad(st[:, 1:2], ((0, 0), (0, NQ - 1), (0, 0))).reshape(NB * NQ, D)
    p2 = jnp.pad(st, ((0, 0), (0, NQ - 2), (0, 0))).reshape(NB * NQ, D)
    (h1s, k_s, v_s, lf_s, qs, gate_s, cv_s) = _pre_sample(x_sample.reshape(NB * NQ, D), p1, p2, w_sample)

    lnewT = jnp.pad(lf_s.reshape(NB, NQ, H), ((0, 0), (0, PG - NQ), (0, 0))).transpose(0, 2, 1)
    og2, ogs = _attention(
        qT, qfT, kT, kfT, vT, gateT, page_table,
        cache_k.transpose(0, 2, 3, 1).reshape(n_pool, D, PG),
        cache_v.transpose(0, 2, 3, 1).reshape(n_pool, D, PG),
        cache_logf.transpose(0, 2, 1),
        k_s.reshape(NB, NQ, D), v_s.reshape(NB, NQ, D), lnewT,
        qs.reshape(NB, NQ, D), gate_s.reshape(NB, NQ, D), _upper_tri(PG))
    y_prompt = _outproj_fm(og2, h1, w_out_b)
    y_sample = _outproj(ogs.reshape(NB * NQ, D), h1s, w_out_b).reshape(NB, NQ, D)

    conv_prompt = tail[:, SUBLANES - (CONV_W - 1):][None]
    conv_sample = cv_s.reshape(NB, NQ, D)[:, NQ - (CONV_W - 1):][None]
    to_bshd = lambda t: t.reshape(B, H, hd, S).transpose(0, 3, 1, 2)
    return (y_prompt, y_sample, conv_prompt, conv_sample,
            to_bshd(kT), to_bshd(vT), lfT.transpose(0, 2, 1),
            k_s.reshape(NB, NQ, H, hd), v_s.reshape(NB, NQ, H, hd), lf_s.reshape(NB, NQ, H))
```
